```python
import jax, jax.numpy as jnp
from jax import lax
import numpy as np

D_MODEL = 1024
BATCH = 8
SEQ = 2048
DEPTH = 2
DEC_BATCH = 32
DEC_SEQ = 8
PAST_LEN = 16384
PAGE_SIZE = 128

GLA_HEADS = 4
GLA_DK = 64
GLA_DV = 128
GLA_RANK = 16
GLA_TAU = 16.0
GLA_CHUNK = 64
SB_HEADS = 4
SB_DIM = 64
SB_BLOCK = 128
SB_BIAS_INIT = -8.0
CONV_CH = 256
CONV_WIDTH = 31
D_FF = 2816
EPS = 1e-6

GLA_QK = GLA_HEADS * GLA_DK
GLA_V = GLA_HEADS * GLA_DV
SB_W = SB_HEADS * SB_DIM
MIX_WIDTH = GLA_V + SB_W + CONV_CH
IN_SIZES = (GLA_QK, GLA_QK, GLA_V, GLA_V, GLA_RANK, SB_W, SB_W, SB_W, 2 * CONV_CH)
IN_COLS = sum(IN_SIZES)

kernel_name = "hymba_gla_stickbreak_conformer_macaron_step"


def rms_norm(x, g):
    xf = x.astype(jnp.float32)
    y = xf * lax.rsqrt(jnp.mean(xf * xf, axis=-1, keepdims=True) + EPS)
    return (y * g.astype(jnp.float32)).astype(x.dtype)


def layer_norm(x, g, b):
    xf = x.astype(jnp.float32)
    mu = jnp.mean(xf, axis=-1, keepdims=True)
    xc = xf - mu
    y = xc * lax.rsqrt(jnp.mean(xc * xc, axis=-1, keepdims=True) + EPS)
    return (y * g.astype(jnp.float32) + b.astype(jnp.float32)).astype(x.dtype)


def half_ffn(x, g_pre, g_post, w_gate, w_up, w_down):
    h = rms_norm(x, g_pre)
    f = (jax.nn.silu(h @ w_gate) * (h @ w_up)) @ w_down
    return x + 0.5 * rms_norm(f, g_post)


def gla_chunked(q, k, v, log_a, s0):
    B, T, H, dk = q.shape
    dv = v.shape[-1]
    C = min(GLA_CHUNK, T)
    n = -(-T // C)
    pad = n * C - T
    padt = lambda a: jnp.pad(a.astype(jnp.float32), ((0, 0), (0, pad), (0, 0), (0, 0)))
    to_chunks = lambda a: padt(a).reshape(B, n, C, H, a.shape[-1]).transpose(1, 0, 3, 2, 4)
    qs, ks, vs, as_ = to_chunks(q), to_chunks(k), to_chunks(v), to_chunks(log_a)
    tri = jnp.tril(jnp.ones((C, C), dtype=bool))

    def step(S, inp):
        qc, kc, vc, ac = inp
        b = jnp.cumsum(ac, axis=2)
        o_inter = jnp.einsum('bhtk,bhkv->bhtv', qc * jnp.exp(b), S)
        diff = b[:, :, :, None, :] - b[:, :, None, :, :]
        decay = jnp.exp(jnp.where(tri[:, :, None], diff, -jnp.inf))
        attn = jnp.einsum('bhtk,bhsk,bhtsk->bhts', qc, kc, decay)
        o = o_inter + jnp.einsum('bhts,bhsv->bhtv', attn, vc)
        b_last = b[:, :, -1:, :]
        S_new = jnp.exp(b_last[:, :, 0, :])[..., None] * S + jnp.einsum(
            'bhsk,bhsv->bhkv', kc * jnp.exp(b_last - b), vc)
        return S_new, o

    S, o = lax.scan(step, s0.astype(jnp.float32), (qs, ks, vs, as_))
    o = o.transpose(1, 0, 3, 2, 4).reshape(B, n * C, H, dv)[:, :T]
    return o, S


def stick_breaking(q, k, v, bias, q_pos, k_pos):
    B, Tq, H, d = q.shape
    qb = min(SB_BLOCK, Tq)
    nb = -(-Tq // qb)
    pad = nb * qb - Tq
    q = jnp.pad(q, ((0, 0), (0, pad), (0, 0), (0, 0)))
    q_pos = jnp.pad(q_pos, (0, pad), constant_values=-1)
    qs = q.reshape(B, nb, qb, H, d).transpose(1, 0, 2, 3, 4)
    ps = q_pos.reshape(nb, qb)
    scale = d ** -0.5
    kf = k.astype(jnp.float32)
    vf = v.astype(jnp.float32)
    bf = bias.astype(jnp.float32)[None, :, None, None]

    def block(args):
        qblk, pblk = args
        z = jnp.einsum('bqhd,bkhd->bhqk', qblk.astype(jnp.float32), kf) * scale + bf
        mask = (k_pos[None, :] < pblk[:, None])[None, None]
        log_1mb = jnp.where(mask, jax.nn.log_sigmoid(-z), 0.0)
        tail = lax.cumsum(log_1mb, axis=3, reverse=True) - log_1mb
        w = jnp.where(mask, jnp.exp(jax.nn.log_sigmoid(z) + tail), 0.0)
        return jnp.einsum('bhqk,bkhd->bqhd', w, vf)

    out = lax.map(block, (qs, ps))
    out = out.transpose(1, 0, 2, 3, 4).reshape(B, nb * qb, H, d)[:, :Tq]
    return out


def causal_dwconv(u, buf, w, b):
    C = u.shape[-1]
    full = jnp.concatenate([buf.astype(u.dtype), u], axis=1)
    y = lax.conv_general_dilated(full, w[:, None, :].astype(u.dtype), window_strides=(1,), padding='VALID',
                                 dimension_numbers=('NWC', 'WIO', 'NWC'), feature_group_count=C)
    return y + b, full[:, -(CONV_WIDTH - 1):]


def mixer(h, past_k, past_v, s0, buf, past_len, p):
    B, T, _ = h.shape
    z = h @ p['w_in']
    offs = np.cumsum(IN_SIZES)[:-1].tolist()
    gq, gk, gv, gg, glr, sq, sk, sv, cu = jnp.split(z, offs, axis=-1)
    q = gq.reshape(B, T, GLA_HEADS, GLA_DK) * (GLA_DK ** -0.5)
    k = gk.reshape(B, T, GLA_HEADS, GLA_DK)
    v = gv.reshape(B, T, GLA_HEADS, GLA_DV)
    log_a = (jax.nn.log_sigmoid((glr @ p['gla_w_a2'] + p['gla_b_a']).astype(jnp.float32)) / GLA_TAU)
    log_a = log_a.reshape(B, T, GLA_HEADS, GLA_DK)
    o_gla, s_new = gla_chunked(q, k, v, log_a, s0)
    o_gla = rms_norm(o_gla, p['gla_norm'].reshape(GLA_HEADS, GLA_DV)).reshape(B, T, GLA_V)
    o_gla = (o_gla * jax.nn.silu(gg.astype(jnp.float32))).astype(h.dtype)
    q_sb = sq.reshape(B, T, SB_HEADS, SB_DIM)
    k_new = sk.reshape(B, T, SB_HEADS, SB_DIM)
    v_new = sv.reshape(B, T, SB_HEADS, SB_DIM)
    k_all = jnp.concatenate([past_k.astype(k_new.dtype), k_new], axis=1)
    v_all = jnp.concatenate([past_v.astype(v_new.dtype), v_new], axis=1)
    q_pos = past_len + jnp.arange(T, dtype=jnp.int32)
    k_pos = jnp.arange(past_len + T, dtype=jnp.int32)
    o_sb = stick_breaking(q_sb, k_all, v_all, p['sb_bias'], q_pos, k_pos).reshape(B, T, SB_W).astype(h.dtype)
    a, gate = jnp.split(cu, 2, axis=-1)
    u = a * jax.nn.sigmoid(gate)
    y, buf_new = causal_dwconv(u, buf, p['conv_w'], p['conv_b'])
    y = jax.nn.silu(layer_norm(y, p['conv_norm_g'], p['conv_norm_b'])).astype(h.dtype)
    mix = jnp.concatenate([o_gla, o_sb, y], axis=-1)
    return mix @ p['w_out'], k_new, v_new, s_new, buf_new


def decoder_layer(x, past_k, past_v, s0, buf, past_len, p):
    x = half_ffn(x, p['ffn1_norm_pre'], p['ffn1_norm_post'], p['ffn1_w_gate'], p['ffn1_w_up'], p['ffn1_w_down'])
    m, k_new, v_new, s_new, buf_new = mixer(rms_norm(x, p['mix_norm_pre']), past_k, past_v, s0, buf, past_len, p)
    x = x + rms_norm(m, p['mix_norm_post'])
    x = half_ffn(x, p['ffn2_norm_pre'], p['ffn2_norm_post'], p['ffn2_w_gate'], p['ffn2_w_up'], p['ffn2_w_down'])
    return x, k_new, v_new, s_new, buf_new


def setup_inputs(seed: int = 0) -> dict:
    key = jax.random.key(seed)
    keys = jax.random.split(key, 32)
    f32 = jnp.float32
    nrm = lambda i, shape, scale: jax.random.normal(keys[i], shape, f32) * scale
    gain = lambda i, shape: 1.0 + 0.02 * jax.random.normal(keys[i], shape, f32)
    n_pages = PAST_LEN // PAGE_SIZE
    n_used = DEC_BATCH * n_pages
    n_pool = n_used + (n_used + 3) // 4
    page_table = jax.random.permutation(keys[4], n_pool)[:n_used].reshape(DEC_BATCH, n_pages).astype(jnp.int32)
    L, D, F = DEPTH, D_MODEL, D_FF
    return {
        'x_prompt': nrm(0, (BATCH, SEQ, D), 1.0),
        'x_sample': nrm(1, (DEC_BATCH, DEC_SEQ, D), 1.0),
        'cache_sb_k': nrm(2, (L, n_pool, PAGE_SIZE, SB_HEADS, SB_DIM), 1.0),
        'cache_sb_v': nrm(3, (L, n_pool, PAGE_SIZE, SB_HEADS, SB_DIM), 1.0),
        'page_table': page_table,
        'state_gla': nrm(5, (L, DEC_BATCH, GLA_HEADS, GLA_DK, GLA_DV), 1.0),
        'state_conv': nrm(6, (L, DEC_BATCH, CONV_WIDTH - 1, CONV_CH), 0.5),
        'ffn1_norm_pre': gain(7, (L, D)),
        'ffn1_norm_post': gain(8, (L, D)),
        'ffn1_w_gate': nrm(9, (L, D, F), D ** -0.5),
        'ffn1_w_up': nrm(10, (L, D, F), D ** -0.5),
        'ffn1_w_down': nrm(11, (L, F, D), F ** -0.5),
        'mix_norm_pre': gain(12, (L, D)),
        'mix_norm_post': gain(13, (L, D)),
        'w_in': nrm(14, (L, D, IN_COLS), D ** -0.5),
        'gla_w_a2': nrm(15, (L, GLA_RANK, GLA_QK), GLA_RANK ** -0.5),
        'gla_b_a': nrm(16, (L, GLA_QK), 0.1),
        'gla_norm': gain(17, (L, GLA_V)),
        'sb_bias': SB_BIAS_INIT + nrm(28, (L, SB_HEADS), 0.1),
        'conv_w': nrm(18, (L, CONV_WIDTH, CONV_CH), CONV_WIDTH ** -0.5),
        'conv_b': nrm(19, (L, CONV_CH), 0.02),
        'conv_norm_g': gain(20, (L, CONV_CH)),
        'conv_norm_b': nrm(21, (L, CONV_CH), 0.02),
        'w_out': nrm(22, (L, MIX_WIDTH, D), MIX_WIDTH ** -0.5),
        'ffn2_norm_pre': gain(23, (L, D)),
        'ffn2_norm_post': gain(24, (L, D)),
        'ffn2_w_gate': nrm(25, (L, D, F), D ** -0.5),
        'ffn2_w_up': nrm(26, (L, D, F), D ** -0.5),
        'ffn2_w_down': nrm(27, (L, F, D), F ** -0.5),
    }


def reference(x_prompt, x_sample, cache_sb_k, cache_sb_v, page_table, state_gla, state_conv,
              ffn1_norm_pre, ffn1_norm_post, ffn1_w_gate, ffn1_w_up, ffn1_w_down,
              mix_norm_pre, mix_norm_post, w_in, gla_w_a2, gla_b_a, gla_norm, sb_bias,
              conv_w, conv_b, conv_norm_g, conv_norm_b, w_out,
              ffn2_norm_pre, ffn2_norm_post, ffn2_w_gate, ffn2_w_up, ffn2_w_down):
    past_len = page_table.shape[1] * cache_sb_k.shape[2]
    B = x_prompt.shape[0]
    DB = x_sample.shape[0]
    xp, xs = x_prompt, x_sample
    pk, pv, ps, pc, sk, sv, ss, sc = [], [], [], [], [], [], [], []
    for l in range(DEPTH):
        p = dict(ffn1_norm_pre=ffn1_norm_pre[l], ffn1_norm_post=ffn1_norm_post[l], ffn1_w_gate=ffn1_w_gate[l],
                 ffn1_w_up=ffn1_w_up[l], ffn1_w_down=ffn1_w_down[l], mix_norm_pre=mix_norm_pre[l],
                 mix_norm_post=mix_norm_post[l], w_in=w_in[l], gla_w_a2=gla_w_a2[l], gla_b_a=gla_b_a[l],
                 gla_norm=gla_norm[l], sb_bias=sb_bias[l], conv_w=conv_w[l], conv_b=conv_b[l],
                 conv_norm_g=conv_norm_g[l], conv_norm_b=conv_norm_b[l], w_out=w_out[l],
                 ffn2_norm_pre=ffn2_norm_pre[l], ffn2_norm_post=ffn2_norm_post[l],
                 ffn2_w_gate=ffn2_w_gate[l], ffn2_w_up=ffn2_w_up[l], ffn2_w_down=ffn2_w_down[l])
        empty = jnp.zeros((B, 0, SB_HEADS, SB_DIM), xp.dtype)
        s0 = jnp.zeros((B, GLA_HEADS, GLA_DK, GLA_DV), jnp.float32)
        buf0 = jnp.zeros((B, CONV_WIDTH - 1, CONV_CH), xp.dtype)
        xp, k1, v1, s1, c1 = decoder_layer(xp, empty, empty, s0, buf0, 0, p)
        pk.append(k1); pv.append(v1); ps.append(s1); pc.append(c1)
        past_k = cache_sb_k[l][page_table].reshape(DB, past_len, SB_HEADS, SB_DIM)
        past_v = cache_sb_v[l][page_table].reshape(DB, past_len, SB_HEADS, SB_DIM)
        xs, k2, v2, s2, c2 = decoder_layer(xs, past_k, past_v, state_gla[l], state_conv[l], past_len, p)
        sk.append(k2); sv.append(v2); ss.append(s2); sc.append(c2)
    return (xp, xs, jnp.stack(pk), jnp.stack(pv), jnp.stack(ps), jnp.stack(pc),
            jnp.stack(sk), jnp.stack(sv), jnp.stack(ss), jnp.stack(sc))
```

```python
import functools

import jax
import jax.numpy as jnp
from jax import lax
from jax.experimental import pallas as pl
from jax.experimental.pallas import tpu as pltpu

F32 = jnp.float32
BF16 = jnp.bfloat16

EPS = 1e-6
GLA_HEADS = 4
GLA_DK = 64
GLA_DV = 128
GLA_RANK = 16
GLA_TAU = 16.0
GLA_CHUNK = 64
SB_HEADS = 4
SB_DIM = 64
CONV_CH = 256
CONV_WIDTH = 31
GLA_QK = GLA_HEADS * GLA_DK
GLA_V = GLA_HEADS * GLA_DV
SB_W = SB_HEADS * SB_DIM

LANES = 128
SB_TILE = 128
HALO = 32
GLA_SAFE_DECAY = 30.0
PAGES_PER_GROUP = 8
VMEM_LIMIT = 48 * 1024 * 1024


def _cparams(sem):
    return pltpu.CompilerParams(dimension_semantics=sem, vmem_limit_bytes=VMEM_LIMIT)


def _rms(x, g):
    return x * lax.rsqrt(jnp.mean(x * x, axis=-1, keepdims=True) + EPS) * g


def _sigmoid(x):
    return 1.0 / (1.0 + jnp.exp(-x))


def _log_sigmoid(x):
    return jnp.minimum(x, 0.0) - jnp.log(1.0 + jnp.exp(-jnp.abs(x)))


def _split2(x):
    hi = x.astype(BF16)
    lo = (x - hi.astype(F32)).astype(BF16)
    return hi, lo


def _split3(x):
    hi = x.astype(BF16)
    r = x - hi.astype(F32)
    mid = r.astype(BF16)
    lo = (r - mid.astype(F32)).astype(BF16)
    return hi, mid, lo


def _dot(a, b):
    return jnp.dot(a, b, preferred_element_type=F32)


def _dot_nt(a, b):
    return lax.dot_general(a, b, (((1,), (1,)), ((), ())), preferred_element_type=F32)


def _dot_tn(a, b):
    return lax.dot_general(a, b, (((0,), (0,)), ((), ())), preferred_element_type=F32)


def _ffn_body(x_ref, gpre_ref, gpost_ref, wg_ref, wu_ref, wd_ref, o_ref, h_scr, acc_scr):
    f = pl.program_id(1)

    @pl.when(f == 0)
    def _():
        h_scr[...] = _rms(x_ref[...], gpre_ref[...]).astype(BF16)
        acc_scr[...] = jnp.zeros_like(acc_scr)

    h = h_scr[...]
    g = _dot(h, wg_ref[...])
    u = _dot(h, wu_ref[...])
    a = (g * _sigmoid(g)) * u
    acc_scr[...] += _dot(a.astype(BF16), wd_ref[...])

    @pl.when(f == pl.num_programs(1) - 1)
    def _():
        o_ref[...] = x_ref[...] + 0.5 * _rms(acc_scr[...], gpost_ref[...])


def _ffn(x, g_pre, g_post, w_gate, w_up, w_down, *, tm, tf):
    n, d = x.shape
    f = w_gate.shape[1]
    return pl.pallas_call(
        _ffn_body,
        grid=(n // tm, f // tf),
        in_specs=[
            pl.BlockSpec((tm, d), lambda i, j: (i, 0)),
            pl.BlockSpec((1, d), lambda i, j: (0, 0)),
            pl.BlockSpec((1, d), lambda i, j: (0, 0)),
            pl.BlockSpec((d, tf), lambda i, j: (0, j)),
            pl.BlockSpec((d, tf), lambda i, j: (0, j)),
            pl.BlockSpec((tf, d), lambda i, j: (j, 0)),
        ],
        out_specs=pl.BlockSpec((tm, d), lambda i, j: (i, 0)),
        out_shape=jax.ShapeDtypeStruct((n, d), F32),
        scratch_shapes=[pltpu.VMEM((tm, d), BF16), pltpu.VMEM((tm, d), F32)],
        compiler_params=_cparams(("parallel", "arbitrary")),
        name="half_ffn",
    )(x, g_pre, g_post, w_gate, w_up, w_down)


_SEG = {}
_off = 0
for _name, _w in (("gq", GLA_QK), ("gk", GLA_QK), ("gv", GLA_V), ("gg", GLA_V), ("sq", SB_W), ("sk", SB_W),
                  ("sv", SB_W), ("ca", CONV_CH), ("cg", CONV_CH)):
    _SEG[_name] = (_off, _off + _w)
    _off += _w
MAIN_COLS = _off


def _mixin_body(x_ref, g_ref, w_ref, wlr_ref, wa2_ref, ba_ref,
                gq_ref, gk_ref, gv_ref, gg_ref, la_ref, sq_ref, sk_ref, sv_ref, u_ref):
    h = _rms(x_ref[...], g_ref[...]).astype(BF16)

    def seg(name):
        lo, hi = _SEG[name]
        return _dot(h, w_ref[:, lo:hi])

    gq_ref[...] = seg("gq") * (GLA_DK ** -0.5)
    gk_ref[...] = seg("gk")
    gv_ref[...] = seg("gv")
    gg_ref[...] = seg("gg")
    sq_ref[...] = seg("sq")
    sk_ref[...] = seg("sk")
    sv_ref[...] = seg("sv")
    u_ref[...] = seg("ca") * _sigmoid(seg("cg"))
    lr = _dot(h, wlr_ref[...])
    xa = _dot(lr.astype(BF16), wa2_ref[...]) + ba_ref[...]
    la_ref[...] = _log_sigmoid(xa) * (1.0 / GLA_TAU)


def _mixin(x, g, w_main, w_lr, w_a2, b_a, *, tm):
    n, d = x.shape
    widths = (GLA_QK, GLA_QK, GLA_V, GLA_V, GLA_QK, SB_W, SB_W, SB_W, CONV_CH)
    const = lambda shape: pl.BlockSpec(shape, lambda i: (0, 0))
    return pl.pallas_call(
        _mixin_body,
        grid=(n // tm,),
        in_specs=[pl.BlockSpec((tm, d), lambda i: (i, 0)), const((1, d)), const(w_main.shape),
                  const(w_lr.shape), const(w_a2.shape), const((1, GLA_QK))],
        out_specs=[pl.BlockSpec((tm, w), lambda i: (i, 0)) for w in widths],
        out_shape=[jax.ShapeDtypeStruct((n, w), F32) for w in widths],
        compiler_params=_cparams(("parallel",)),
        name="mixer_in",
    )(x, g, w_main, w_lr, w_a2, b_a)


def _gla_body(q_ref, k_ref, v_ref, la_ref, gg_ref, s0_ref, gn_ref, o_ref, sout_ref, st_scr, b_scr, oi_scr, *, C):
    c = pl.program_id(1)

    @pl.when(c == 0)
    def _():
        st_scr[...] = s0_ref[0]

    q = q_ref[0]
    k = k_ref[0]
    v = v_ref[0]
    la = la_ref[0]
    row = lax.broadcasted_iota(jnp.int32, (C, C), 0)
    col = lax.broadcasted_iota(jnp.int32, (C, C), 1)
    causal = row >= col
    tri = jnp.where(causal, 1.0, 0.0).astype(BF16)
    hi, mid, lo = _split3(la)
    b = _dot(tri, jnp.concatenate([hi, mid, lo], axis=1))
    b = b[:, :GLA_QK] + b[:, GLA_QK:2 * GLA_QK] + b[:, 2 * GLA_QK:]
    b_last = b[C - 1:C, :]
    lane_head = lax.broadcasted_iota(jnp.int32, (C, GLA_QK), 1) // GLA_DK
    st = st_scr[...]
    qb = q * jnp.exp(b)
    kd = k * jnp.exp(b_last - b)
    total_decay = jnp.max(-b_last)

    for h in range(GLA_HEADS):
        qb_h = jnp.where(lane_head == h, qb, 0.0).astype(BF16)
        oi_scr[:, h * GLA_DV:(h + 1) * GLA_DV] = _dot_nt(qb_h, st.astype(BF16))
    upd = jnp.zeros_like(st)
    for h in range(GLA_HEADS):
        kd_h = jnp.where(lane_head == h, kd, 0.0).astype(BF16)
        upd = upd + _dot_tn(v[:, h * GLA_DV:(h + 1) * GLA_DV].astype(BF16), kd_h)
    st_scr[...] = st * jnp.exp(b_last) + upd

    @pl.when(total_decay <= GLA_SAFE_DECAY)
    def _():
        kb = (k * jnp.exp(-b)).astype(BF16)
        for h in range(GLA_HEADS):
            qb_h = jnp.where(lane_head == h, qb, 0.0).astype(BF16)
            attn = jnp.where(causal, _dot_nt(qb_h, kb), 0.0)
            sl = slice(h * GLA_DV, (h + 1) * GLA_DV)
            oi_scr[:, sl] += _dot(attn.astype(BF16), v[:, sl].astype(BF16))

    @pl.when(total_decay > GLA_SAFE_DECAY)
    def _():
        b_scr[...] = b
        rhead = lax.broadcasted_iota(jnp.int32, (GLA_QK, GLA_V), 0) // GLA_DK
        chead = lax.broadcasted_iota(jnp.int32, (GLA_QK, GLA_V), 1) // GLA_DV
        spread = jnp.where(rhead == chead, 1.0, 0.0).astype(BF16)
        trow = lax.broadcasted_iota(jnp.int32, (C, GLA_QK), 0)

        def body(s, acc):
            bs = b_scr[pl.ds(s, 1), :]
            ks = k_ref[0, pl.ds(s, 1), :]
            vs = v_ref[0, pl.ds(s, 1), :]
            p = q * ks * jnp.exp(jnp.minimum(b - bs, 0.0))
            p = jnp.where(trow >= s, p, 0.0)
            p_hi, p_mid, p_lo = _split3(p)
            r = _dot(p_hi, spread) + _dot(p_mid, spread) + _dot(p_lo, spread)
            return acc + r * vs

        oi_scr[...] += lax.fori_loop(0, C, body, jnp.zeros((C, GLA_V), F32))

    o = oi_scr[...]
    gg = gg_ref[0]
    for h in range(GLA_HEADS):
        sl = slice(h * GLA_DV, (h + 1) * GLA_DV)
        o_ref[0, :, sl] = _rms(o[:, sl], gn_ref[:, sl]) * (gg[:, sl] * _sigmoid(gg[:, sl]))

    @pl.when(c == pl.num_programs(1) - 1)
    def _():
        sout_ref[0] = st_scr[...]


def _gla(q, k, v, la, gg, s0_t, gnorm):
    bsz, t_real, _ = q.shape
    if t_real < 16:
        padt = lambda a: jnp.pad(a, ((0, 0), (0, 16 - t_real), (0, 0)))
        q, k, v, la, gg = (padt(a) for a in (q, k, v, la, gg))
    t = q.shape[1]
    c = min(GLA_CHUNK, t)
    assert t % c == 0
    tok = lambda w: pl.BlockSpec((1, c, w), lambda b, i: (b, i, 0))
    st = pl.BlockSpec((1, GLA_DV, GLA_QK), lambda b, i: (b, 0, 0))
    o, st_new = pl.pallas_call(
        functools.partial(_gla_body, C=c),
        grid=(bsz, t // c),
        in_specs=[tok(GLA_QK), tok(GLA_QK), tok(GLA_V), tok(GLA_QK), tok(GLA_V), st,
                  pl.BlockSpec((1, GLA_V), lambda b, i: (0, 0))],
        out_specs=[tok(GLA_V), st],
        out_shape=[jax.ShapeDtypeStruct((bsz, t, GLA_V), F32), jax.ShapeDtypeStruct((bsz, GLA_DV, GLA_QK), F32)],
        scratch_shapes=[pltpu.VMEM((GLA_DV, GLA_QK), F32), pltpu.VMEM((c, GLA_QK), F32),
                        pltpu.VMEM((c, GLA_V), F32)],
        compiler_params=_cparams(("parallel", "arbitrary")),
        name="gla",
    )(q, k, v, la, gg, s0_t, gnorm)
    return o[:, :t_real], st_new


def _state_to_t(s):
    b = s.shape[0]
    return jnp.transpose(s, (0, 3, 1, 2)).reshape(b, GLA_DV, GLA_QK)


def _state_from_t(st):
    b = st.shape[0]
    return jnp.transpose(st.reshape(b, GLA_DV, GLA_HEADS, GLA_DK), (0, 2, 3, 1))


def _sb_weights(z, tail_mat_ref, carry, mask):
    ls = _log_sigmoid(z)
    l1m = ls - z
    if mask is not None:
        l1m = jnp.where(mask, l1m, 0.0)
    hi, lo = _split2(l1m)
    cs = _dot(jnp.concatenate([hi, lo], axis=1), tail_mat_ref[...])
    w = jnp.exp(ls + cs[:, :LANES] + carry)
    if mask is not None:
        w = jnp.where(mask, w, 0.0)
    return w, carry + cs[:, LANES:]


def _tail_matrix():
    src = jnp.arange(2 * LANES)[:, None] % LANES
    dst = jnp.arange(2 * LANES)[None, :]
    return jnp.where((dst >= LANES) | (src > dst), 1.0, 0.0).astype(BF16)


def _sb_prompt_body(bias_ref, q_ref, k_ref, v_ref, tm_ref, o_ref, kbf_scr, vbd_scr, qbd_scr, carry_scr, acc_scr,
                    *, T):
    i = pl.program_id(1)
    nkb = T // SB_TILE
    lane_head = lax.broadcasted_iota(jnp.int32, (SB_TILE, SB_W), 1) // SB_DIM

    @pl.when(i == 0)
    def _():
        kbf_scr[...] = k_ref[0].astype(BF16)
        for jb in range(nkb):
            vb = v_ref[0, jb * SB_TILE:(jb + 1) * SB_TILE, :]
            for h in range(SB_HEADS):
                vbd_scr[jb, h * SB_TILE:(h + 1) * SB_TILE, :] = jnp.where(lane_head == h, vb, 0.0).astype(BF16)

    q = q_ref[0] * (SB_DIM ** -0.5)
    for h in range(SB_HEADS):
        qbd_scr[h] = jnp.where(lane_head == h, q, 0.0).astype(BF16)
    carry_scr[...] = jnp.zeros_like(carry_scr)
    acc_scr[...] = jnp.zeros_like(acc_scr)
    trow = lax.broadcasted_iota(jnp.int32, (SB_TILE, SB_TILE), 0)
    scol = lax.broadcasted_iota(jnp.int32, (SB_TILE, SB_TILE), 1)
    strict = scol < trow

    def tile(j, mask):
        kblk = kbf_scr[pl.ds(pl.multiple_of(j * SB_TILE, SB_TILE), SB_TILE), :]
        ws = []
        for h in range(SB_HEADS):
            z = _dot_nt(qbd_scr[h], kblk) + bias_ref[h]
            w, carry_scr[h] = _sb_weights(z, tm_ref, carry_scr[h], mask)
            ws.append(w.astype(BF16))
        acc_scr[...] += _dot(jnp.concatenate(ws, axis=1), vbd_scr[j])

    tile(i, strict)

    def body(jj, _):
        tile(i - 1 - jj, None)
        return 0

    lax.fori_loop(0, i, body, 0)
    o_ref[0] = acc_scr[...]


def _sb_prompt(q, k, v, bias, tail_mat):
    bsz, t, w = q.shape
    nq = t // SB_TILE
    full = pl.BlockSpec((1, t, w), lambda b, i: (b, 0, 0))
    qt = pl.BlockSpec((1, SB_TILE, w), lambda b, i: (b, i, 0))
    return pl.pallas_call(
        functools.partial(_sb_prompt_body, T=t),
        grid=(bsz, nq),
        in_specs=[pl.BlockSpec(memory_space=pltpu.SMEM), qt, full, full,
                  pl.BlockSpec(tail_mat.shape, lambda b, i: (0, 0))],
        out_specs=qt,
        out_shape=jax.ShapeDtypeStruct((bsz, t, w), F32),
        scratch_shapes=[pltpu.VMEM((t, w), BF16), pltpu.VMEM((nq, SB_HEADS * SB_TILE, w), BF16),
                        pltpu.VMEM((SB_HEADS, SB_TILE, w), BF16), pltpu.VMEM((SB_HEADS, SB_TILE, LANES), F32),
                        pltpu.VMEM((SB_TILE, w), F32)],
        compiler_params=_cparams(("parallel", "arbitrary")),
        name="sb_prompt",
    )(bias, q, k, v, tail_mat)


def _sb_decode_body(pt_ref, bias_ref, q_ref, kn_ref, vn_ref, tm_ref, kc_hbm, vc_hbm, o_ref,
                    kbuf, vbuf, sem, knew_scr, vnew_scr, carry_scr, acc_scr, *, TQ, NP, G, LAYER):
    b = pl.program_id(0)
    nb = pl.num_programs(0)
    ng = NP // G
    R = SB_HEADS * TQ

    def copies(bb, grp, slot):
        out = []
        for p in range(G):
            page = pt_ref[bb, grp * G + p]
            keys = pl.ds(p * SB_TILE, SB_TILE)
            out.append(pltpu.make_async_copy(kc_hbm.at[LAYER, page], kbuf.at[slot, :, keys], sem.at[0, slot]))
            out.append(pltpu.make_async_copy(vc_hbm.at[LAYER, page], vbuf.at[slot, :, keys], sem.at[1, slot]))
        return out

    @pl.when(b == 0)
    def _():
        for cp in copies(0, ng - 1, 0):
            cp.start()

    lane_head = lax.broadcasted_iota(jnp.int32, (TQ, SB_W), 1) // SB_DIM
    q = q_ref[0] * (SB_DIM ** -0.5)
    qbd = jnp.concatenate([jnp.where(lane_head == h, q, 0.0) for h in range(SB_HEADS)], axis=0).astype(BF16)
    bias = jnp.concatenate([jnp.full((TQ, LANES), bias_ref[h], F32) for h in range(SB_HEADS)], axis=0)

    knew_scr[...] = jnp.zeros_like(knew_scr)
    vnew_scr[...] = jnp.zeros_like(vnew_scr)
    knew_scr[0:TQ, :] = kn_ref[0]
    vnew_scr[0:TQ, :] = vn_ref[0]
    trow = lax.broadcasted_iota(jnp.int32, (R, LANES), 0) % TQ
    scol = lax.broadcasted_iota(jnp.int32, (R, LANES), 1)
    z = _dot_nt(qbd, knew_scr[...].astype(BF16)) + bias
    w, carry = _sb_weights(z, tm_ref, jnp.zeros((R, LANES), F32), scol < trow)
    carry_scr[...] = carry
    acc_scr[...] = _dot(w.astype(BF16), vnew_scr[...].astype(BF16))

    def group(r, _):
        step = b * ng + r
        slot = step % 2
        grp = ng - 1 - r
        for cp in copies(b, grp, slot):
            cp.wait()

        @pl.when(r + 1 < ng)
        def _():
            for cp in copies(b, grp - 1, 1 - slot):
                cp.start()

        @pl.when((r + 1 == ng) & (b + 1 < nb))
        def _():
            for cp in copies(b + 1, ng - 1, 1 - slot):
                cp.start()

        kg = kbuf[slot].astype(BF16)
        vg = vbuf[slot].astype(BF16)
        zg = _dot(qbd, kg)
        zrows = jnp.concatenate([zg[:, p * LANES:(p + 1) * LANES] + bias for p in range(G)], axis=0)
        ls = _log_sigmoid(zrows)
        hi, lo = _split2(ls - zrows)
        cs = _dot(jnp.concatenate([hi, lo], axis=1), tm_ref[...])
        carry = carry_scr[...]
        ws = [None] * G
        for p in reversed(range(G)):
            rs = slice(p * R, (p + 1) * R)
            ws[p] = jnp.exp(ls[rs] + cs[rs, :LANES] + carry).astype(BF16)
            carry = carry + cs[rs, LANES:]
        carry_scr[...] = carry
        acc_scr[...] += _dot_nt(jnp.concatenate(ws, axis=1), vg)
        return 0

    lax.fori_loop(0, ng, group, 0)
    acc = acc_scr[...]
    out = jnp.zeros((TQ, SB_W), F32)
    for h in range(SB_HEADS):
        out = out + jnp.where(lane_head == h, acc[h * TQ:(h + 1) * TQ, :], 0.0)
    o_ref[0] = out


def _sb_decode(q, k_new, v_new, bias, tail_mat, cache_k, cache_v, page_table, layer):
    db, tq, w = q.shape
    npages = page_table.shape[1]
    g = PAGES_PER_GROUP
    assert npages % g == 0 and cache_k.shape[2:] == (w, SB_TILE)
    tok = pl.BlockSpec((1, tq, w), lambda b, pt: (b, 0, 0))
    grid_spec = pltpu.PrefetchScalarGridSpec(
        num_scalar_prefetch=1,
        grid=(db,),
        in_specs=[pl.BlockSpec(memory_space=pltpu.SMEM), tok, tok, tok,
                  pl.BlockSpec(tail_mat.shape, lambda b, pt: (0, 0)),
                  pl.BlockSpec(memory_space=pl.ANY), pl.BlockSpec(memory_space=pl.ANY)],
        out_specs=tok,
        scratch_shapes=[pltpu.VMEM((2, w, g * SB_TILE), F32), pltpu.VMEM((2, w, g * SB_TILE), F32),
                        pltpu.SemaphoreType.DMA((2, 2)),
                        pltpu.VMEM((SB_TILE, w), F32), pltpu.VMEM((SB_TILE, w), F32),
                        pltpu.VMEM((SB_HEADS * tq, LANES), F32), pltpu.VMEM((SB_HEADS * tq, w), F32)],
    )
    return pl.pallas_call(
        functools.partial(_sb_decode_body, TQ=tq, NP=npages, G=g, LAYER=layer),
        grid_spec=grid_spec,
        out_shape=jax.ShapeDtypeStruct((db, tq, w), F32),
        compiler_params=_cparams(("arbitrary",)),
        name="sb_decode",
    )(page_table, bias, q, k_new, v_new, tail_mat, cache_k, cache_v)


def _conv_body(u_ref, prev_ref, buf_ref, w_ref, cb_ref, lg_ref, lb_ref, y_ref, bufo_ref, win_scr, *, RT, NT):
    i = pl.program_id(1)
    hist = CONV_WIDTH - 1
    first = HALO - hist

    @pl.when(i == 0)
    def _():
        win_scr[first:HALO, :] = buf_ref[0]

    if NT > 1:
        @pl.when(i > 0)
        def _():
            win_scr[0:HALO, :] = prev_ref[0]

    win_scr[HALO:HALO + RT, :] = u_ref[0]
    acc = jnp.zeros((RT, CONV_CH), F32)
    for j in range(CONV_WIDTH):
        acc = acc + win_scr[first + j:first + j + RT, :] * w_ref[j:j + 1, :]
    y = acc + cb_ref[...]
    mu = jnp.mean(y, axis=-1, keepdims=True)
    yc = y - mu
    yn = yc * lax.rsqrt(jnp.mean(yc * yc, axis=-1, keepdims=True) + EPS) * lg_ref[...] + lb_ref[...]
    y_ref[0] = yn * _sigmoid(yn)

    @pl.when(i == pl.num_programs(1) - 1)
    def _():
        bufo_ref[0] = win_scr[first + RT:HALO + RT, :]


def _conv(u, buf, w, cb, lg, lb):
    bsz, t, ch = u.shape
    hist = CONV_WIDTH - 1
    rt = min(t, 128)
    nt = t // rt
    assert t % rt == 0 and (nt == 1 or rt % HALO == 0)
    per = rt // HALO if nt > 1 else 1
    tile = pl.BlockSpec((1, rt, ch), lambda b, i: (b, i, 0))
    prev_rows = HALO if nt > 1 else rt
    prev = pl.BlockSpec((1, prev_rows, ch), lambda b, i: (b, jnp.maximum(i * per - 1, 0), 0))
    state = pl.BlockSpec((1, hist, ch), lambda b, i: (b, 0, 0))
    vec = lambda r: pl.BlockSpec((r, ch), lambda b, i: (0, 0))
    return pl.pallas_call(
        functools.partial(_conv_body, RT=rt, NT=nt),
        grid=(bsz, nt),
        in_specs=[tile, prev, state, vec(CONV_WIDTH), vec(1), vec(1), vec(1)],
        out_specs=[tile, state],
        out_shape=[jax.ShapeDtypeStruct((bsz, t, ch), F32), jax.ShapeDtypeStruct((bsz, hist, ch), F32)],
        scratch_shapes=[pltpu.VMEM((HALO + rt, ch), F32)],
        compiler_params=_cparams(("parallel", "arbitrary")),
        name="conv_branch",
    )(u, u, buf, w, cb, lg, lb)


def _mixout_body(x_ref, og_ref, osb_ref, y_ref, w_ref, g_ref, o_ref):
    m = _dot(og_ref[...].astype(BF16), w_ref[0:GLA_V, :])
    m = m + _dot(osb_ref[...].astype(BF16), w_ref[GLA_V:GLA_V + SB_W, :])
    m = m + _dot(y_ref[...].astype(BF16), w_ref[GLA_V + SB_W:, :])
    o_ref[...] = x_ref[...] + _rms(m, g_ref[...])


def _mixout(x, og, osb, y, w_out, g, *, tm):
    n, d = x.shape
    tok = lambda w: pl.BlockSpec((tm, w), lambda i: (i, 0))
    return pl.pallas_call(
        _mixout_body,
        grid=(n // tm,),
        in_specs=[tok(d), tok(GLA_V), tok(SB_W), tok(CONV_CH),
                  pl.BlockSpec(w_out.shape, lambda i: (0, 0)), pl.BlockSpec((1, d), lambda i: (0, 0))],
        out_specs=tok(d),
        out_shape=jax.ShapeDtypeStruct((n, d), F32),
        compiler_params=_cparams(("parallel",)),
        name="mixer_out",
    )(x, og, osb, y, w_out, g)


def _layer_weights(l, w):
    d = w["w_in"].shape[1]
    w_in = w["w_in"][l]
    sizes = (GLA_QK, GLA_QK, GLA_V, GLA_V, GLA_RANK, SB_W, SB_W, SB_W, 2 * CONV_CH)
    offs = [0]
    for s in sizes:
        offs.append(offs[-1] + s)
    main = jnp.concatenate([w_in[:, :offs[4]], w_in[:, offs[5]:]], axis=1).astype(BF16)
    w_lr = jnp.zeros((d, LANES), F32).at[:, :GLA_RANK].set(w_in[:, offs[4]:offs[5]]).astype(BF16)
    w_a2 = jnp.zeros((LANES, GLA_QK), F32).at[:GLA_RANK, :].set(w["gla_w_a2"][l]).astype(BF16)
    row = lambda name: w[name][l][None, :]
    return dict(
        ffn1=(row("ffn1_norm_pre"), row("ffn1_norm_post"), w["ffn1_w_gate"][l].astype(BF16),
              w["ffn1_w_up"][l].astype(BF16), w["ffn1_w_down"][l].astype(BF16)),
        ffn2=(row("ffn2_norm_pre"), row("ffn2_norm_post"), w["ffn2_w_gate"][l].astype(BF16),
              w["ffn2_w_up"][l].astype(BF16), w["ffn2_w_down"][l].astype(BF16)),
        mixin=(row("mix_norm_pre"), main, w_lr, w_a2, row("gla_b_a")),
        gla_norm=row("gla_norm"),
        sb_bias=w["sb_bias"][l],
        conv=(w["conv_w"][l], row("conv_b"), row("conv_norm_g"), row("conv_norm_b")),
        w_out=w["w_out"][l].astype(BF16),
        mix_post=row("mix_norm_post"),
    )


def _token_tile(n):
    for tm in (1024, 512, 256, 128, 64, 32, 16, 8):
        if n % tm == 0:
            return tm
    raise ValueError(n)


def _ffn_tile(f):
    for tf in (256, 128):
        if f % tf == 0:
            return tf
    return f


def _group_layer(x, lw, s0, buf, tail_mat, attend):
    bsz, t, d = x.shape
    n = bsz * t
    tm = _token_tile(n)
    tf = _ffn_tile(lw["ffn1"][2].shape[1])
    x2 = _ffn(x.reshape(n, d), *lw["ffn1"], tm=tm, tf=tf)
    gq, gk, gv, gg, la, sq, sk, sv, u = _mixin(x2, *lw["mixin"], tm=min(tm, 512))
    seq = lambda a: a.reshape(bsz, t, a.shape[-1])
    o_gla, st = _gla(seq(gq), seq(gk), seq(gv), seq(la), seq(gg), _state_to_t(s0), lw["gla_norm"])
    o_sb = attend(seq(sq), seq(sk), seq(sv))
    y, buf_new = _conv(seq(u), buf, *lw["conv"])
    x3 = _mixout(x2, o_gla.reshape(n, GLA_V), o_sb.reshape(n, SB_W), y.reshape(n, CONV_CH), lw["w_out"],
                 lw["mix_post"], tm=min(tm, 512))
    x4 = _ffn(x3, *lw["ffn2"], tm=tm, tf=tf)
    heads = lambda a: a.reshape(bsz, t, SB_HEADS, SB_DIM)
    return x4.reshape(bsz, t, d), heads(sk), heads(sv), _state_from_t(st), buf_new


def kernel(x_prompt, x_sample, cache_sb_k, cache_sb_v, page_table, state_gla, state_conv, ffn1_norm_pre, ffn1_norm_post, ffn1_w_gate, ffn1_w_up, ffn1_w_down, mix_norm_pre, mix_norm_post, w_in, gla_w_a2, gla_b_a, gla_norm, sb_bias, conv_w, conv_b, conv_norm_g, conv_norm_b, w_out, ffn2_norm_pre, ffn2_norm_post, ffn2_w_gate, ffn2_w_up, ffn2_w_down):
    weights = dict(ffn1_norm_pre=ffn1_norm_pre, ffn1_norm_post=ffn1_norm_post, ffn1_w_gate=ffn1_w_gate,
                   ffn1_w_up=ffn1_w_up, ffn1_w_down=ffn1_w_down, mix_norm_pre=mix_norm_pre,
                   mix_norm_post=mix_norm_post, w_in=w_in, gla_w_a2=gla_w_a2, gla_b_a=gla_b_a, gla_norm=gla_norm,
                   sb_bias=sb_bias, conv_w=conv_w, conv_b=conv_b, conv_norm_g=conv_norm_g,
                   conv_norm_b=conv_norm_b, w_out=w_out, ffn2_norm_pre=ffn2_norm_pre,
                   ffn2_norm_post=ffn2_norm_post, ffn2_w_gate=ffn2_w_gate, ffn2_w_up=ffn2_w_up,
                   ffn2_w_down=ffn2_w_down)
    depth = w_in.shape[0]
    bsz = x_prompt.shape[0]
    n_pool, page = cache_sb_k.shape[1], cache_sb_k.shape[2]
    tail_mat = _tail_matrix()
    ck = jnp.transpose(cache_sb_k, (0, 1, 3, 4, 2)).reshape(depth, n_pool, SB_W, page)
    cv = jnp.transpose(cache_sb_v, (0, 1, 3, 4, 2)).reshape(depth, n_pool, SB_W, page)
    xp, xs = x_prompt, x_sample
    outs = [[] for _ in range(8)]
    for l in range(depth):
        lw = _layer_weights(l, weights)
        s0 = jnp.zeros((bsz, GLA_HEADS, GLA_DK, GLA_DV), F32)
        buf0 = jnp.zeros((bsz, CONV_WIDTH - 1, CONV_CH), F32)
        prompt_attend = lambda q, k, v: _sb_prompt(q, k, v, lw["sb_bias"], tail_mat)
        xp, k1, v1, s1, c1 = _group_layer(xp, lw, s0, buf0, tail_mat, prompt_attend)
        decode_attend = lambda q, k, v, l=l: _sb_decode(q, k, v, lw["sb_bias"], tail_mat, ck, cv, page_table, l)
        xs, k2, v2, s2, c2 = _group_layer(xs, lw, state_gla[l], state_conv[l], tail_mat, decode_attend)
        for lst, val in zip(outs, (k1, v1, s1, c1, k2, v2, s2, c2)):
            lst.append(val)
    return (xp, xs) + tuple(jnp.stack(o) for o in outs)
```

```python
import functools
import math

import jax
import jax.numpy as jnp
from jax import lax
from jax.experimental import pallas as pl
from jax.experimental.pallas import tpu as pltpu

F32 = jnp.float32
BF16 = jnp.bfloat16

EPS = 1e-6
GLA_HEADS = 4
GLA_DK = 64
GLA_DV = 128
GLA_RANK = 16
GLA_TAU = 16.0
GLA_CHUNK = 64
SB_HEADS = 4
SB_DIM = 64
CONV_CH = 256
CONV_WIDTH = 31
GLA_QK = GLA_HEADS * GLA_DK
GLA_V = GLA_HEADS * GLA_DV
SB_W = SB_HEADS * SB_DIM

LANES = 128
SB_TILE = 128
HALO = 32
GLA_SAFE_DECAY = 30.0
GLA_SEQS_PER_STEP = 4
PAGES_PER_GROUP = 16
DECODE_SLOTS = 4
VMEM_LIMIT = 48 * 1024 * 1024


def _cparams(sem):
    return pltpu.CompilerParams(dimension_semantics=sem, vmem_limit_bytes=VMEM_LIMIT)


def _rms(x, g):
    return x * lax.rsqrt(jnp.mean(x * x, axis=-1, keepdims=True) + EPS) * g


def _sigmoid(x):
    return 1.0 / (1.0 + jnp.exp(-x))


def _log_sigmoid(x):
    return jnp.minimum(x, 0.0) - jnp.log(1.0 + jnp.exp(-jnp.abs(x)))


def _split2(x):
    hi = x.astype(BF16)
    lo = (x - hi.astype(F32)).astype(BF16)
    return hi, lo


def _split3(x):
    hi = x.astype(BF16)
    r = x - hi.astype(F32)
    mid = r.astype(BF16)
    lo = (r - mid.astype(F32)).astype(BF16)
    return hi, mid, lo


def _dot(a, b):
    return jnp.dot(a, b, preferred_element_type=F32)


def _dot_nt(a, b):
    return lax.dot_general(a, b, (((1,), (1,)), ((), ())), preferred_element_type=F32)


def _dot_tn(a, b):
    return lax.dot_general(a, b, (((0,), (0,)), ((), ())), preferred_element_type=F32)


def _ffn_body(x_ref, gpre_ref, gpost_ref, wg_ref, wu_ref, wd_ref, o_ref, h_scr, acc_scr):
    f = pl.program_id(1)

    @pl.when(f == 0)
    def _():
        h_scr[...] = _rms(x_ref[...], gpre_ref[...]).astype(BF16)
        acc_scr[...] = jnp.zeros_like(acc_scr)

    h = h_scr[...]
    g = _dot(h, wg_ref[...])
    u = _dot(h, wu_ref[...])
    a = (g * _sigmoid(g)) * u
    acc_scr[...] += _dot(a.astype(BF16), wd_ref[...])

    @pl.when(f == pl.num_programs(1) - 1)
    def _():
        o_ref[...] = x_ref[...] + 0.5 * _rms(acc_scr[...], gpost_ref[...])


def _ffn(x, g_pre, g_post, w_gate, w_up, w_down, *, tm, tf):
    n, d = x.shape
    f = w_gate.shape[1]
    return pl.pallas_call(
        _ffn_body,
        grid=(n // tm, f // tf),
        in_specs=[
            pl.BlockSpec((tm, d), lambda i, j: (i, 0)),
            pl.BlockSpec((1, d), lambda i, j: (0, 0)),
            pl.BlockSpec((1, d), lambda i, j: (0, 0)),
            pl.BlockSpec((d, tf), lambda i, j: (0, j)),
            pl.BlockSpec((d, tf), lambda i, j: (0, j)),
            pl.BlockSpec((tf, d), lambda i, j: (j, 0)),
        ],
        out_specs=pl.BlockSpec((tm, d), lambda i, j: (i, 0)),
        out_shape=jax.ShapeDtypeStruct((n, d), F32),
        scratch_shapes=[pltpu.VMEM((tm, d), BF16), pltpu.VMEM((tm, d), F32)],
        compiler_params=_cparams(("parallel", "arbitrary")),
        name="half_ffn",
    )(x, g_pre, g_post, w_gate, w_up, w_down)


_SEG = {}
_off = 0
for _name, _w in (("gq", GLA_QK), ("gk", GLA_QK), ("gv", GLA_V), ("gg", GLA_V), ("sq", SB_W), ("sk", SB_W),
                  ("sv", SB_W), ("ca", CONV_CH), ("cg", CONV_CH)):
    _SEG[_name] = (_off, _off + _w)
    _off += _w
MAIN_COLS = _off


def _mixin_body(x_ref, g_ref, w_ref, wlr_ref, wa2_ref, ba_ref,
                gq_ref, gk_ref, gv_ref, gg_ref, la_ref, sq_ref, sk_ref, sv_ref, u_ref):
    h = _rms(x_ref[...], g_ref[...]).astype(BF16)

    def seg(name):
        lo, hi = _SEG[name]
        return _dot(h, w_ref[:, lo:hi])

    gq_ref[...] = seg("gq") * (GLA_DK ** -0.5)
    gk_ref[...] = seg("gk")
    gv_ref[...] = seg("gv")
    gg_ref[...] = seg("gg")
    sq_ref[...] = seg("sq")
    sk_ref[...] = seg("sk")
    sv_ref[...] = seg("sv")
    u_ref[...] = seg("ca") * _sigmoid(seg("cg"))
    lr = _dot(h, wlr_ref[...])
    xa = _dot(lr.astype(BF16), wa2_ref[...]) + ba_ref[...]
    la_ref[...] = _log_sigmoid(xa) * (1.0 / GLA_TAU)


def _mixin(x, g, w_main, w_lr, w_a2, b_a, *, tm):
    n, d = x.shape
    widths = (GLA_QK, GLA_QK, GLA_V, GLA_V, GLA_QK, SB_W, SB_W, SB_W, CONV_CH)
    const = lambda shape: pl.BlockSpec(shape, lambda i: (0, 0))
    return pl.pallas_call(
        _mixin_body,
        grid=(n // tm,),
        in_specs=[pl.BlockSpec((tm, d), lambda i: (i, 0)), const((1, d)), const(w_main.shape),
                  const(w_lr.shape), const(w_a2.shape), const((1, GLA_QK))],
        out_specs=[pl.BlockSpec((tm, w), lambda i: (i, 0)) for w in widths],
        out_shape=[jax.ShapeDtypeStruct((n, w), F32) for w in widths],
        compiler_params=_cparams(("parallel",)),
        name="mixer_in",
    )(x, g, w_main, w_lr, w_a2, b_a)


def _gla_body(q_ref, k_ref, v_ref, la_ref, gg_ref, s0_ref, gn_ref, o_ref, sout_ref, st_scr, b_scr, oi_scr,
              *, C, NB):
    c = pl.program_id(1)

    @pl.when(c == 0)
    def _():
        st_scr[...] = s0_ref[...]

    row = lax.broadcasted_iota(jnp.int32, (C, C), 0)
    col = lax.broadcasted_iota(jnp.int32, (C, C), 1)
    causal = row >= col
    tri = jnp.where(causal, 1.0, 0.0).astype(BF16)
    lane_head = lax.broadcasted_iota(jnp.int32, (C, GLA_QK), 1) // GLA_DK
    heads = [slice(h * GLA_DV, (h + 1) * GLA_DV) for h in range(GLA_HEADS)]

    def head_rows(x, h):
        return jnp.where(lane_head == h, x, 0.0).astype(BF16)

    decays = []
    for n in range(NB):
        q, k, v = q_ref[n], k_ref[n], v_ref[n]
        hi, mid, lo = _split3(la_ref[n])
        b = _dot(tri, jnp.concatenate([hi, mid, lo], axis=1))
        b = b[:, :GLA_QK] + b[:, GLA_QK:2 * GLA_QK] + b[:, 2 * GLA_QK:]
        b_scr[n] = b
        b_last = b[C - 1:C, :]
        decays.append(jnp.max(-b_last))
        st = st_scr[n]
        qb = q * jnp.exp(b)
        kd = k * jnp.exp(b_last - b)
        for h, sl in enumerate(heads):
            oi_scr[n, :, sl] = _dot_nt(head_rows(qb, h), st.astype(BF16))
        upd = jnp.zeros_like(st)
        for h, sl in enumerate(heads):
            upd = upd + _dot_tn(v[:, sl].astype(BF16), head_rows(kd, h))
        st_scr[n] = st * jnp.exp(b_last) + upd
    total_decay = functools.reduce(jnp.maximum, decays)

    @pl.when(total_decay <= GLA_SAFE_DECAY)
    def _():
        for n in range(NB):
            q, k, v, b = q_ref[n], k_ref[n], v_ref[n], b_scr[n]
            qb = q * jnp.exp(b)
            kb = (k * jnp.exp(-b)).astype(BF16)
            for h, sl in enumerate(heads):
                attn = jnp.where(causal, _dot_nt(head_rows(qb, h), kb), 0.0)
                oi_scr[n, :, sl] += _dot(attn.astype(BF16), v[:, sl].astype(BF16))

    @pl.when(total_decay > GLA_SAFE_DECAY)
    def _():
        rhead = lax.broadcasted_iota(jnp.int32, (GLA_QK, GLA_V), 0) // GLA_DK
        chead = lax.broadcasted_iota(jnp.int32, (GLA_QK, GLA_V), 1) // GLA_DV
        spread = jnp.where(rhead == chead, 1.0, 0.0).astype(BF16)
        trow = lax.broadcasted_iota(jnp.int32, (C, GLA_QK), 0)
        for n in range(NB):
            q, b = q_ref[n], b_scr[n]

            def body(s, acc):
                bs = b_scr[n, pl.ds(s, 1), :]
                ks = k_ref[n, pl.ds(s, 1), :]
                vs = v_ref[n, pl.ds(s, 1), :]
                p = q * ks * jnp.exp(jnp.minimum(b - bs, 0.0))
                p = jnp.where(trow >= s, p, 0.0)
                p_hi, p_mid, p_lo = _split3(p)
                r = _dot(p_hi, spread) + _dot(p_mid, spread) + _dot(p_lo, spread)
                return acc + r * vs

            oi_scr[n] += lax.fori_loop(0, C, body, jnp.zeros((C, GLA_V), F32))

    for n in range(NB):
        o, gg = oi_scr[n], gg_ref[n]
        for sl in heads:
            o_ref[n, :, sl] = _rms(o[:, sl], gn_ref[:, sl]) * (gg[:, sl] * _sigmoid(gg[:, sl]))

    @pl.when(c == pl.num_programs(1) - 1)
    def _():
        sout_ref[...] = st_scr[...]


def _gla(q, k, v, la, gg, s0_t, gnorm):
    bsz, t_real, _ = q.shape
    if t_real < 16:
        padt = lambda a: jnp.pad(a, ((0, 0), (0, 16 - t_real), (0, 0)))
        q, k, v, la, gg = (padt(a) for a in (q, k, v, la, gg))
    t = q.shape[1]
    c = min(GLA_CHUNK, t)
    nb = GLA_SEQS_PER_STEP if bsz % GLA_SEQS_PER_STEP == 0 else 1
    assert t % c == 0
    tok = lambda w: pl.BlockSpec((nb, c, w), lambda b, i: (b, i, 0))
    st = pl.BlockSpec((nb, GLA_DV, GLA_QK), lambda b, i: (b, 0, 0))
    o, st_new = pl.pallas_call(
        functools.partial(_gla_body, C=c, NB=nb),
        grid=(bsz // nb, t // c),
        in_specs=[tok(GLA_QK), tok(GLA_QK), tok(GLA_V), tok(GLA_QK), tok(GLA_V), st,
                  pl.BlockSpec((1, GLA_V), lambda b, i: (0, 0))],
        out_specs=[tok(GLA_V), st],
        out_shape=[jax.ShapeDtypeStruct((bsz, t, GLA_V), F32), jax.ShapeDtypeStruct((bsz, GLA_DV, GLA_QK), F32)],
        scratch_shapes=[pltpu.VMEM((nb, GLA_DV, GLA_QK), F32), pltpu.VMEM((nb, c, GLA_QK), F32),
                        pltpu.VMEM((nb, c, GLA_V), F32)],
        compiler_params=_cparams(("parallel", "arbitrary")),
        name="gla",
    )(q, k, v, la, gg, s0_t, gnorm)
    return o[:, :t_real], st_new


def _state_to_t(s):
    b = s.shape[0]
    return jnp.transpose(s, (0, 3, 1, 2)).reshape(b, GLA_DV, GLA_QK)


def _state_from_t(st):
    b = st.shape[0]
    return jnp.transpose(st.reshape(b, GLA_DV, GLA_HEADS, GLA_DK), (0, 2, 3, 1))


def _sb_weights(z, tail_mat_ref, carry, mask):
    ls = _log_sigmoid(z)
    l1m = ls - z
    if mask is not None:
        l1m = jnp.where(mask, l1m, 0.0)
    hi, lo = _split2(l1m)
    cs = _dot(jnp.concatenate([hi, lo], axis=1), tail_mat_ref[...])
    w = jnp.exp(ls + cs[:, :LANES] + carry)
    if mask is not None:
        w = jnp.where(mask, w, 0.0)
    return w, carry + cs[:, LANES:]


def _tail_matrix():
    src = jnp.arange(2 * LANES)[:, None] % LANES
    dst = jnp.arange(2 * LANES)[None, :]
    return jnp.where((dst >= LANES) | (src > dst), 1.0, 0.0).astype(BF16)


def _sb_prompt_body(bias_ref, q_ref, k_ref, v_ref, tm_ref, o_ref, kbf_scr, vbd_scr, qbd_scr, carry_scr, acc_scr,
                    *, T):
    i = pl.program_id(1)
    nkb = T // SB_TILE
    R = SB_HEADS * SB_TILE
    lane_head = lax.broadcasted_iota(jnp.int32, (SB_TILE, SB_W), 1) // SB_DIM

    @pl.when(i == 0)
    def _():
        kbf_scr[...] = k_ref[0].astype(BF16)
        for jb in range(nkb):
            vb = v_ref[0, jb * SB_TILE:(jb + 1) * SB_TILE, :]
            for h in range(SB_HEADS):
                vbd_scr[jb, h * SB_TILE:(h + 1) * SB_TILE, :] = jnp.where(lane_head == h, vb, 0.0).astype(BF16)

    q = q_ref[0] * (SB_DIM ** -0.5)
    for h in range(SB_HEADS):
        qbd_scr[h * SB_TILE:(h + 1) * SB_TILE, :] = jnp.where(lane_head == h, q, 0.0).astype(BF16)
    carry_scr[...] = jnp.zeros_like(carry_scr)
    acc_scr[...] = jnp.zeros_like(acc_scr)

    def tile(j, mask):
        kblk = kbf_scr[pl.ds(pl.multiple_of(j * SB_TILE, SB_TILE), SB_TILE), :]
        z = _dot_nt(qbd_scr[...], kblk) + bias_ref[...]
        w, carry_scr[...] = _sb_weights(z, tm_ref, carry_scr[...], mask)
        wb = w.astype(BF16)
        wcat = jnp.concatenate([wb[h * SB_TILE:(h + 1) * SB_TILE] for h in range(SB_HEADS)], axis=1)
        acc_scr[...] += _dot(wcat, vbd_scr[j])

    trow = lax.broadcasted_iota(jnp.int32, (R, SB_TILE), 0) % SB_TILE
    scol = lax.broadcasted_iota(jnp.int32, (R, SB_TILE), 1)
    tile(i, scol < trow)

    def pair(jj, _):
        tile(i - 1 - 2 * jj, None)
        tile(i - 2 - 2 * jj, None)
        return 0

    lax.fori_loop(0, i // 2, pair, 0)

    @pl.when(i % 2 == 1)
    def _():
        tile(0, None)

    o_ref[0] = acc_scr[...]


def _sb_prompt(q, k, v, bias_rows, tail_mat):
    bsz, t, w = q.shape
    nq = t // SB_TILE
    full = pl.BlockSpec((1, t, w), lambda b, i: (b, 0, 0))
    qt = pl.BlockSpec((1, SB_TILE, w), lambda b, i: (b, i, 0))
    const = lambda a: pl.BlockSpec(a.shape, lambda b, i: (0, 0))
    return pl.pallas_call(
        functools.partial(_sb_prompt_body, T=t),
        grid=(bsz, nq),
        in_specs=[const(bias_rows), qt, full, full, const(tail_mat)],
        out_specs=qt,
        out_shape=jax.ShapeDtypeStruct((bsz, t, w), F32),
        scratch_shapes=[pltpu.VMEM((t, w), BF16), pltpu.VMEM((nq, SB_HEADS * SB_TILE, w), BF16),
                        pltpu.VMEM((SB_HEADS * SB_TILE, w), BF16), pltpu.VMEM((SB_HEADS * SB_TILE, LANES), F32),
                        pltpu.VMEM((SB_TILE, w), F32)],
        compiler_params=_cparams(("parallel", "arbitrary")),
        name="sb_prompt",
    )(bias_rows, q, k, v, tail_mat)


def _sb_decode_body(pt_ref, bias_ref, q_ref, kn_ref, vn_ref, tm_ref, kc_hbm, vc_hbm, o_ref,
                    kbuf, vbuf, sem, knew_scr, vnew_scr, carry_scr, acc_scr, *, TQ, NP, G, NB, NSLOT, LAYER):
    b = pl.program_id(0)
    nb = NB
    ng = NP // G
    R = SB_HEADS * TQ

    def copies(bb, grp, slot):
        out = []
        for p in range(G):
            page = pt_ref[bb, grp * G + p]
            keys = pl.ds(p * SB_TILE, SB_TILE)
            out.append(pltpu.make_async_copy(kc_hbm.at[LAYER, page], kbuf.at[slot, :, keys], sem.at[0, slot]))
            out.append(pltpu.make_async_copy(vc_hbm.at[LAYER, page], vbuf.at[slot, :, keys], sem.at[1, slot]))
        return out

    def start_group(step):
        for cp in copies(step // ng, ng - 1 - step % ng, step % NSLOT):
            cp.start()

    @pl.when(b == 0)
    def _():
        for s in range(min(NSLOT - 1, nb * ng)):
            start_group(s)

    lane_head = lax.broadcasted_iota(jnp.int32, (TQ, SB_W), 1) // SB_DIM
    q = q_ref[0] * (SB_DIM ** -0.5)
    qbd = jnp.concatenate([jnp.where(lane_head == h, q, 0.0) for h in range(SB_HEADS)], axis=0).astype(BF16)
    bias = jnp.concatenate([jnp.full((TQ, LANES), bias_ref[h], F32) for h in range(SB_HEADS)], axis=0)

    knew_scr[...] = jnp.zeros_like(knew_scr)
    vnew_scr[...] = jnp.zeros_like(vnew_scr)
    knew_scr[0:TQ, :] = kn_ref[0]
    vnew_scr[0:TQ, :] = vn_ref[0]
    trow = lax.broadcasted_iota(jnp.int32, (R, LANES), 0) % TQ
    scol = lax.broadcasted_iota(jnp.int32, (R, LANES), 1)
    z = _dot_nt(qbd, knew_scr[...].astype(BF16)) + bias
    w, carry = _sb_weights(z, tm_ref, jnp.zeros((R, LANES), F32), scol < trow)
    carry_scr[...] = carry
    acc_scr[...] = _dot(w.astype(BF16), vnew_scr[...].astype(BF16))

    def group(r, _):
        step = b * ng + r
        slot = step % NSLOT
        for cp in copies(b, ng - 1 - r, slot):
            cp.wait()

        @pl.when(step + NSLOT - 1 < nb * ng)
        def _():
            start_group(step + NSLOT - 1)

        kg = kbuf[slot].astype(BF16)
        vg = vbuf[slot].astype(BF16)
        zg = _dot(qbd, kg)
        zrows = jnp.concatenate([zg[:, p * LANES:(p + 1) * LANES] + bias for p in range(G)], axis=0)
        ls = _log_sigmoid(zrows)
        hi, lo = _split2(ls - zrows)
        cs = _dot(jnp.concatenate([hi, lo], axis=1), tm_ref[...])
        carry = carry_scr[...]
        ws = [None] * G
        for p in reversed(range(G)):
            rs = slice(p * R, (p + 1) * R)
            ws[p] = jnp.exp(ls[rs] + cs[rs, :LANES] + carry).astype(BF16)
            carry = carry + cs[rs, LANES:]
        carry_scr[...] = carry
        acc_scr[...] += _dot_nt(jnp.concatenate(ws, axis=1), vg)
        return 0

    lax.fori_loop(0, ng, group, 0)
    acc = acc_scr[...]
    out = jnp.zeros((TQ, SB_W), F32)
    for h in range(SB_HEADS):
        out = out + jnp.where(lane_head == h, acc[h * TQ:(h + 1) * TQ, :], 0.0)
    o_ref[0] = out


def _sb_decode(q, k_new, v_new, bias, tail_mat, cache_k, cache_v, page_table, layer):
    db, tq, w = q.shape
    npages = page_table.shape[1]
    g = math.gcd(npages, PAGES_PER_GROUP)
    assert cache_k.shape[2:] == (w, SB_TILE)
    tok = pl.BlockSpec((1, tq, w), lambda b, pt: (b, 0, 0))
    grid_spec = pltpu.PrefetchScalarGridSpec(
        num_scalar_prefetch=1,
        grid=(db,),
        in_specs=[pl.BlockSpec(memory_space=pltpu.SMEM), tok, tok, tok,
                  pl.BlockSpec(tail_mat.shape, lambda b, pt: (0, 0)),
                  pl.BlockSpec(memory_space=pl.ANY), pl.BlockSpec(memory_space=pl.ANY)],
        out_specs=tok,
        scratch_shapes=[pltpu.VMEM((DECODE_SLOTS, w, g * SB_TILE), F32), pltpu.VMEM((DECODE_SLOTS, w, g * SB_TILE), F32),
                        pltpu.SemaphoreType.DMA((2, DECODE_SLOTS)),
                        pltpu.VMEM((SB_TILE, w), F32), pltpu.VMEM((SB_TILE, w), F32),
                        pltpu.VMEM((SB_HEADS * tq, LANES), F32), pltpu.VMEM((SB_HEADS * tq, w), F32)],
    )
    return pl.pallas_call(
        functools.partial(_sb_decode_body, TQ=tq, NP=npages, G=g, NB=db, NSLOT=DECODE_SLOTS,
                          LAYER=layer),
        grid_spec=grid_spec,
        out_shape=jax.ShapeDtypeStruct((db, tq, w), F32),
        compiler_params=_cparams(("arbitrary",)),
        name="sb_decode",
    )(page_table, bias, q, k_new, v_new, tail_mat, cache_k, cache_v)


def _conv_body(u_ref, prev_ref, buf_ref, w_ref, cb_ref, lg_ref, lb_ref, y_ref, bufo_ref, win_scr, *, RT, NT):
    i = pl.program_id(1)
    hist = CONV_WIDTH - 1
    first = HALO - hist

    @pl.when(i == 0)
    def _():
        win_scr[first:HALO, :] = buf_ref[0]

    if NT > 1:
        @pl.when(i > 0)
        def _():
            win_scr[0:HALO, :] = prev_ref[0]

    win_scr[HALO:HALO + RT, :] = u_ref[0]
    acc = jnp.zeros((RT, CONV_CH), F32)
    for j in range(CONV_WIDTH):
        acc = acc + win_scr[first + j:first + j + RT, :] * w_ref[j:j + 1, :]
    y = acc + cb_ref[...]
    mu = jnp.mean(y, axis=-1, keepdims=True)
    yc = y - mu
    yn = yc * lax.rsqrt(jnp.mean(yc * yc, axis=-1, keepdims=True) + EPS) * lg_ref[...] + lb_ref[...]
    y_ref[0] = yn * _sigmoid(yn)

    @pl.when(i == pl.num_programs(1) - 1)
    def _():
        bufo_ref[0] = win_scr[first + RT:HALO + RT, :]


def _conv(u, buf, w, cb, lg, lb):
    bsz, t, ch = u.shape
    hist = CONV_WIDTH - 1
    rt = min(t, 128)
    nt = t // rt
    assert t % rt == 0 and (nt == 1 or rt % HALO == 0)
    per = rt // HALO if nt > 1 else 1
    tile = pl.BlockSpec((1, rt, ch), lambda b, i: (b, i, 0))
    prev_rows = HALO if nt > 1 else rt
    prev = pl.BlockSpec((1, prev_rows, ch), lambda b, i: (b, jnp.maximum(i * per - 1, 0), 0))
    state = pl.BlockSpec((1, hist, ch), lambda b, i: (b, 0, 0))
    vec = lambda r: pl.BlockSpec((r, ch), lambda b, i: (0, 0))
    return pl.pallas_call(
        functools.partial(_conv_body, RT=rt, NT=nt),
        grid=(bsz, nt),
        in_specs=[tile, prev, state, vec(CONV_WIDTH), vec(1), vec(1), vec(1)],
        out_specs=[tile, state],
        out_shape=[jax.ShapeDtypeStruct((bsz, t, ch), F32), jax.ShapeDtypeStruct((bsz, hist, ch), F32)],
        scratch_shapes=[pltpu.VMEM((HALO + rt, ch), F32)],
        compiler_params=_cparams(("parallel", "arbitrary")),
        name="conv_branch",
    )(u, u, buf, w, cb, lg, lb)


def _mixout_body(x_ref, og_ref, osb_ref, y_ref, w_ref, g_ref, o_ref):
    m = _dot(og_ref[...].astype(BF16), w_ref[0:GLA_V, :])
    m = m + _dot(osb_ref[...].astype(BF16), w_ref[GLA_V:GLA_V + SB_W, :])
    m = m + _dot(y_ref[...].astype(BF16), w_ref[GLA_V + SB_W:, :])
    o_ref[...] = x_ref[...] + _rms(m, g_ref[...])


def _mixout(x, og, osb, y, w_out, g, *, tm):
    n, d = x.shape
    tok = lambda w: pl.BlockSpec((tm, w), lambda i: (i, 0))
    return pl.pallas_call(
        _mixout_body,
        grid=(n // tm,),
        in_specs=[tok(d), tok(GLA_V), tok(SB_W), tok(CONV_CH),
                  pl.BlockSpec(w_out.shape, lambda i: (0, 0)), pl.BlockSpec((1, d), lambda i: (0, 0))],
        out_specs=tok(d),
        out_shape=jax.ShapeDtypeStruct((n, d), F32),
        compiler_params=_cparams(("parallel",)),
        name="mixer_out",
    )(x, og, osb, y, w_out, g)


def _layer_weights(l, w):
    d = w["w_in"].shape[1]
    w_in = w["w_in"][l]
    sizes = (GLA_QK, GLA_QK, GLA_V, GLA_V, GLA_RANK, SB_W, SB_W, SB_W, 2 * CONV_CH)
    offs = [0]
    for s in sizes:
        offs.append(offs[-1] + s)
    main = jnp.concatenate([w_in[:, :offs[4]], w_in[:, offs[5]:]], axis=1).astype(BF16)
    w_lr = jnp.zeros((d, LANES), F32).at[:, :GLA_RANK].set(w_in[:, offs[4]:offs[5]]).astype(BF16)
    w_a2 = jnp.zeros((LANES, GLA_QK), F32).at[:GLA_RANK, :].set(w["gla_w_a2"][l]).astype(BF16)
    row = lambda name: w[name][l][None, :]
    return dict(
        ffn1=(row("ffn1_norm_pre"), row("ffn1_norm_post"), w["ffn1_w_gate"][l].astype(BF16),
              w["ffn1_w_up"][l].astype(BF16), w["ffn1_w_down"][l].astype(BF16)),
        ffn2=(row("ffn2_norm_pre"), row("ffn2_norm_post"), w["ffn2_w_gate"][l].astype(BF16),
              w["ffn2_w_up"][l].astype(BF16), w["ffn2_w_down"][l].astype(BF16)),
        mixin=(row("mix_norm_pre"), main, w_lr, w_a2, row("gla_b_a")),
        gla_norm=row("gla_norm"),
        sb_bias=w["sb_bias"][l],
        sb_bias_rows=jnp.broadcast_to(jnp.repeat(w["sb_bias"][l], SB_TILE)[:, None], (SB_HEADS * SB_TILE, LANES)),
        conv=(w["conv_w"][l], row("conv_b"), row("conv_norm_g"), row("conv_norm_b")),
        w_out=w["w_out"][l].astype(BF16),
        mix_post=row("mix_norm_post"),
    )


def _token_tile(n):
    for tm in (1024, 512, 256, 128, 64, 32, 16, 8):
        if n % tm == 0:
            return tm
    raise ValueError(n)


def _ffn_tile(f):
    for tf in (256, 128):
        if f % tf == 0:
            return tf
    return f


def _group_layer(x, lw, s0, buf, tail_mat, attend):
    bsz, t, d = x.shape
    n = bsz * t
    tm = _token_tile(n)
    tf = _ffn_tile(lw["ffn1"][2].shape[1])
    x2 = _ffn(x.reshape(n, d), *lw["ffn1"], tm=tm, tf=tf)
    gq, gk, gv, gg, la, sq, sk, sv, u = _mixin(x2, *lw["mixin"], tm=min(tm, 512))
    seq = lambda a: a.reshape(bsz, t, a.shape[-1])
    o_gla, st = _gla(seq(gq), seq(gk), seq(gv), seq(la), seq(gg), _state_to_t(s0), lw["gla_norm"])
    o_sb = attend(seq(sq), seq(sk), seq(sv))
    y, buf_new = _conv(seq(u), buf, *lw["conv"])
    x3 = _mixout(x2, o_gla.reshape(n, GLA_V), o_sb.reshape(n, SB_W), y.reshape(n, CONV_CH), lw["w_out"],
                 lw["mix_post"], tm=min(tm, 512))
    x4 = _ffn(x3, *lw["ffn2"], tm=tm, tf=tf)
    heads = lambda a: a.reshape(bsz, t, SB_HEADS, SB_DIM)
    return x4.reshape(bsz, t, d), heads(sk), heads(sv), _state_from_t(st), buf_new


def kernel(x_prompt, x_sample, cache_sb_k, cache_sb_v, page_table, state_gla, state_conv, ffn1_norm_pre, ffn1_norm_post, ffn1_w_gate, ffn1_w_up, ffn1_w_down, mix_norm_pre, mix_norm_post, w_in, gla_w_a2, gla_b_a, gla_norm, sb_bias, conv_w, conv_b, conv_norm_g, conv_norm_b, w_out, ffn2_norm_pre, ffn2_norm_post, ffn2_w_gate, ffn2_w_up, ffn2_w_down):
    weights = dict(ffn1_norm_pre=ffn1_norm_pre, ffn1_norm_post=ffn1_norm_post, ffn1_w_gate=ffn1_w_gate,
                   ffn1_w_up=ffn1_w_up, ffn1_w_down=ffn1_w_down, mix_norm_pre=mix_norm_pre,
                   mix_norm_post=mix_norm_post, w_in=w_in, gla_w_a2=gla_w_a2, gla_b_a=gla_b_a, gla_norm=gla_norm,
                   sb_bias=sb_bias, conv_w=conv_w, conv_b=conv_b, conv_norm_g=conv_norm_g,
                   conv_norm_b=conv_norm_b, w_out=w_out, ffn2_norm_pre=ffn2_norm_pre,
                   ffn2_norm_post=ffn2_norm_post, ffn2_w_gate=ffn2_w_gate, ffn2_w_up=ffn2_w_up,
                   ffn2_w_down=ffn2_w_down)
    depth = w_in.shape[0]
    bsz = x_prompt.shape[0]
    n_pool, page = cache_sb_k.shape[1], cache_sb_k.shape[2]
    tail_mat = _tail_matrix()
    ck = jnp.transpose(cache_sb_k, (0, 1, 3, 4, 2)).reshape(depth, n_pool, SB_W, page)
    cv = jnp.transpose(cache_sb_v, (0, 1, 3, 4, 2)).reshape(depth, n_pool, SB_W, page)
    xp, xs = x_prompt, x_sample
    outs = [[] for _ in range(8)]
    for l in range(depth):
        lw = _layer_weights(l, weights)
        s0 = jnp.zeros((bsz, GLA_HEADS, GLA_DK, GLA_DV), F32)
        buf0 = jnp.zeros((bsz, CONV_WIDTH - 1, CONV_CH), F32)
        prompt_attend = lambda q, k, v: _sb_prompt(q, k, v, lw["sb_bias_rows"], tail_mat)
        xp, k1, v1, s1, c1 = _group_layer(xp, lw, s0, buf0, tail_mat, prompt_attend)
        decode_attend = lambda q, k, v, l=l: _sb_decode(q, k, v, lw["sb_bias"], tail_mat, ck, cv, page_table, l)
        xs, k2, v2, s2, c2 = _group_layer(xs, lw, state_gla[l], state_conv[l], tail_mat, decode_attend)
        for lst, val in zip(outs, (k1, v1, s1, c1, k2, v2, s2, c2)):
            lst.append(val)
    return (xp, xs) + tuple(jnp.stack(o) for o in outs)
```

```python
import functools
import math

import jax
import jax.numpy as jnp
from jax import lax
from jax.experimental import pallas as pl
from jax.experimental.pallas import tpu as pltpu

F32 = jnp.float32
BF16 = jnp.bfloat16

EPS = 1e-6
GLA_HEADS = 4
GLA_DK = 64
GLA_DV = 128
GLA_RANK = 16
GLA_TAU = 16.0
GLA_CHUNK = 128
SB_HEADS = 4
SB_DIM = 64
CONV_CH = 256
CONV_WIDTH = 31
GLA_QK = GLA_HEADS * GLA_DK
GLA_V = GLA_HEADS * GLA_DV
SB_W = SB_HEADS * SB_DIM

LANES = 128
SUBLANES = 8
SB_TILE = 128
HALO = 32
CONV_TILE = 256
CONV_SUBTILE = 64
GLA_SAFE_DECAY = 60.0
GLA_SEQS_PER_STEP = 4
PAGES_PER_GROUP = 16
DECODE_SLOTS = 4
VMEM_LIMIT = 48 * 1024 * 1024


def _cparams(sem):
    return pltpu.CompilerParams(dimension_semantics=sem, vmem_limit_bytes=VMEM_LIMIT)


def _rms(x, g):
    return x * lax.rsqrt(jnp.mean(x * x, axis=-1, keepdims=True) + EPS) * g


def _sigmoid(x):
    return 1.0 / (1.0 + jnp.exp(-x))


def _log_sigmoid(x):
    return jnp.minimum(x, 0.0) - jnp.log(1.0 + jnp.exp(-jnp.abs(x)))


def _split2(x):
    hi = x.astype(BF16)
    lo = (x - hi.astype(F32)).astype(BF16)
    return hi, lo


def _split3(x):
    hi = x.astype(BF16)
    r = x - hi.astype(F32)
    mid = r.astype(BF16)
    lo = (r - mid.astype(F32)).astype(BF16)
    return hi, mid, lo


def _dot(a, b):
    return jnp.dot(a, b, preferred_element_type=F32)


def _dot_nt(a, b):
    return lax.dot_general(a, b, (((1,), (1,)), ((), ())), preferred_element_type=F32)


def _dot_tn(a, b):
    return lax.dot_general(a, b, (((0,), (0,)), ((), ())), preferred_element_type=F32)


def _ffn_body(x_ref, gpre_ref, gpost_ref, wg_ref, wu_ref, wd_ref, o_ref, h_scr, acc_scr):
    f = pl.program_id(1)

    @pl.when(f == 0)
    def _():
        h_scr[...] = _rms(x_ref[...], gpre_ref[...]).astype(BF16)
        acc_scr[...] = jnp.zeros_like(acc_scr)

    h = h_scr[...]
    g = _dot(h, wg_ref[...])
    u = _dot(h, wu_ref[...])
    a = (g * _sigmoid(g)) * u
    acc_scr[...] += _dot(a.astype(BF16), wd_ref[...])

    @pl.when(f == pl.num_programs(1) - 1)
    def _():
        o_ref[...] = x_ref[...] + 0.5 * _rms(acc_scr[...], gpost_ref[...])


def _ffn(x, g_pre, g_post, w_gate, w_up, w_down, *, tm, tf):
    n, d = x.shape
    f = w_gate.shape[1]
    return pl.pallas_call(
        _ffn_body,
        grid=(n // tm, f // tf),
        in_specs=[
            pl.BlockSpec((tm, d), lambda i, j: (i, 0)),
            pl.BlockSpec((1, d), lambda i, j: (0, 0)),
            pl.BlockSpec((1, d), lambda i, j: (0, 0)),
            pl.BlockSpec((d, tf), lambda i, j: (0, j)),
            pl.BlockSpec((d, tf), lambda i, j: (0, j)),
            pl.BlockSpec((tf, d), lambda i, j: (j, 0)),
        ],
        out_specs=pl.BlockSpec((tm, d), lambda i, j: (i, 0)),
        out_shape=jax.ShapeDtypeStruct((n, d), F32),
        scratch_shapes=[pltpu.VMEM((tm, d), BF16), pltpu.VMEM((tm, d), F32)],
        compiler_params=_cparams(("parallel", "arbitrary")),
        name="half_ffn",
    )(x, g_pre, g_post, w_gate, w_up, w_down)


_SEG = {}
_off = 0
for _name, _w in (("gq", GLA_QK), ("gk", GLA_QK), ("gv", GLA_V), ("gg", GLA_V), ("sq", SB_W), ("sk", SB_W),
                  ("sv", SB_W), ("ca", CONV_CH), ("cg", CONV_CH)):
    _SEG[_name] = (_off, _off + _w)
    _off += _w
MAIN_COLS = _off


def _mixin_body(x_ref, g_ref, w_ref, wlr_ref, wa2_ref, ba_ref,
                gq_ref, gk_ref, gv_ref, gg_ref, la_ref, sq_ref, sk_ref, sv_ref, u_ref):
    h = _rms(x_ref[...], g_ref[...]).astype(BF16)

    def seg(name):
        lo, hi = _SEG[name]
        return _dot(h, w_ref[:, lo:hi])

    gq_ref[...] = seg("gq") * (GLA_DK ** -0.5)
    gk_ref[...] = seg("gk")
    gv_ref[...] = seg("gv")
    gg_ref[...] = seg("gg")
    sq_ref[...] = seg("sq")
    sk_ref[...] = seg("sk")
    sv_ref[...] = seg("sv")
    u_ref[...] = seg("ca") * _sigmoid(seg("cg"))
    lr = _dot(h, wlr_ref[...])
    xa = _dot(lr.astype(BF16), wa2_ref[...]) + ba_ref[...]
    la_ref[...] = _log_sigmoid(xa) * (1.0 / GLA_TAU)


def _mixin(x, g, w_main, w_lr, w_a2, b_a, *, tm):
    n, d = x.shape
    widths = (GLA_QK, GLA_QK, GLA_V, GLA_V, GLA_QK, SB_W, SB_W, SB_W, CONV_CH)
    const = lambda shape: pl.BlockSpec(shape, lambda i: (0, 0))
    return pl.pallas_call(
        _mixin_body,
        grid=(n // tm,),
        in_specs=[pl.BlockSpec((tm, d), lambda i: (i, 0)), const((1, d)), const(w_main.shape),
                  const(w_lr.shape), const(w_a2.shape), const((1, GLA_QK))],
        out_specs=[pl.BlockSpec((tm, w), lambda i: (i, 0)) for w in widths],
        out_shape=[jax.ShapeDtypeStruct((n, w), F32) for w in widths],
        compiler_params=_cparams(("parallel",)),
        name="mixer_in",
    )(x, g, w_main, w_lr, w_a2, b_a)


def _gla_body(q_ref, k_ref, v_ref, la_ref, gg_ref, s0_ref, gn_ref, o_ref, sout_ref, st_scr, b_scr, oi_scr,
              *, C, NB):
    c = pl.program_id(1)

    @pl.when(c == 0)
    def _():
        st_scr[...] = s0_ref[...]

    row = lax.broadcasted_iota(jnp.int32, (C, C), 0)
    col = lax.broadcasted_iota(jnp.int32, (C, C), 1)
    causal = row >= col
    tri = jnp.where(causal, 1.0, 0.0).astype(BF16)
    lane_head = lax.broadcasted_iota(jnp.int32, (C, GLA_QK), 1) // GLA_DK
    heads = [slice(h * GLA_DV, (h + 1) * GLA_DV) for h in range(GLA_HEADS)]

    def head_rows(x, h):
        return jnp.where(lane_head == h, x, 0.0).astype(BF16)

    decays = []
    for n in range(NB):
        q, k, v = q_ref[n], k_ref[n], v_ref[n]
        hi, mid, lo = _split3(la_ref[n])
        b = _dot(tri, jnp.concatenate([hi, mid, lo], axis=1))
        b = b[:, :GLA_QK] + b[:, GLA_QK:2 * GLA_QK] + b[:, 2 * GLA_QK:]
        b_scr[n] = b
        b_last = b[C - 1:C, :]
        decays.append(jnp.max(-b_last))
        st = st_scr[n]
        qb = q * jnp.exp(b)
        kd = k * jnp.exp(b_last - b)
        for h, sl in enumerate(heads):
            oi_scr[n, :, sl] = _dot_nt(head_rows(qb, h), st.astype(BF16))
        upd = jnp.zeros_like(st)
        for h, sl in enumerate(heads):
            upd = upd + _dot_tn(v[:, sl].astype(BF16), head_rows(kd, h))
        st_scr[n] = st * jnp.exp(b_last) + upd
    total_decay = functools.reduce(jnp.maximum, decays)

    @pl.when(total_decay <= GLA_SAFE_DECAY)
    def _():
        for n in range(NB):
            q, k, v, b = q_ref[n], k_ref[n], v_ref[n], b_scr[n]
            qb = q * jnp.exp(b)
            kb = (k * jnp.exp(-b)).astype(BF16)
            for h, sl in enumerate(heads):
                attn = jnp.where(causal, _dot_nt(head_rows(qb, h), kb), 0.0)
                oi_scr[n, :, sl] += _dot(attn.astype(BF16), v[:, sl].astype(BF16))

    @pl.when(total_decay > GLA_SAFE_DECAY)
    def _():
        rhead = lax.broadcasted_iota(jnp.int32, (GLA_QK, GLA_V), 0) // GLA_DK
        chead = lax.broadcasted_iota(jnp.int32, (GLA_QK, GLA_V), 1) // GLA_DV
        spread = jnp.where(rhead == chead, 1.0, 0.0).astype(BF16)
        trow = lax.broadcasted_iota(jnp.int32, (C, GLA_QK), 0)
        for n in range(NB):
            q, b = q_ref[n], b_scr[n]

            def body(s, acc):
                bs = b_scr[n, pl.ds(s, 1), :]
                ks = k_ref[n, pl.ds(s, 1), :]
                vs = v_ref[n, pl.ds(s, 1), :]
                p = q * ks * jnp.exp(jnp.minimum(b - bs, 0.0))
                p = jnp.where(trow >= s, p, 0.0)
                p_hi, p_mid, p_lo = _split3(p)
                r = _dot(p_hi, spread) + _dot(p_mid, spread) + _dot(p_lo, spread)
                return acc + r * vs

            oi_scr[n] += lax.fori_loop(0, C, body, jnp.zeros((C, GLA_V), F32))

    for n in range(NB):
        o, gg = oi_scr[n], gg_ref[n]
        for sl in heads:
            o_ref[n, :, sl] = _rms(o[:, sl], gn_ref[:, sl]) * (gg[:, sl] * _sigmoid(gg[:, sl]))

    @pl.when(c == pl.num_programs(1) - 1)
    def _():
        sout_ref[...] = st_scr[...]


def _gla(q, k, v, la, gg, s0_t, gnorm):
    bsz, t_real, _ = q.shape
    if t_real < 16:
        padt = lambda a: jnp.pad(a, ((0, 0), (0, 16 - t_real), (0, 0)))
        q, k, v, la, gg = (padt(a) for a in (q, k, v, la, gg))
    t = q.shape[1]
    c = min(GLA_CHUNK, t)
    nb = GLA_SEQS_PER_STEP if bsz % GLA_SEQS_PER_STEP == 0 else 1
    assert t % c == 0
    tok = lambda w: pl.BlockSpec((nb, c, w), lambda b, i: (b, i, 0))
    st = pl.BlockSpec((nb, GLA_DV, GLA_QK), lambda b, i: (b, 0, 0))
    o, st_new = pl.pallas_call(
        functools.partial(_gla_body, C=c, NB=nb),
        grid=(bsz // nb, t // c),
        in_specs=[tok(GLA_QK), tok(GLA_QK), tok(GLA_V), tok(GLA_QK), tok(GLA_V), st,
                  pl.BlockSpec((1, GLA_V), lambda b, i: (0, 0))],
        out_specs=[tok(GLA_V), st],
        out_shape=[jax.ShapeDtypeStruct((bsz, t, GLA_V), F32), jax.ShapeDtypeStruct((bsz, GLA_DV, GLA_QK), F32)],
        scratch_shapes=[pltpu.VMEM((nb, GLA_DV, GLA_QK), F32), pltpu.VMEM((nb, c, GLA_QK), F32),
                        pltpu.VMEM((nb, c, GLA_V), F32)],
        compiler_params=_cparams(("parallel", "arbitrary")),
        name="gla",
    )(q, k, v, la, gg, s0_t, gnorm)
    return o[:, :t_real], st_new


def _state_to_t(s):
    b = s.shape[0]
    return jnp.transpose(s, (0, 3, 1, 2)).reshape(b, GLA_DV, GLA_QK)


def _state_from_t(st):
    b = st.shape[0]
    return jnp.transpose(st.reshape(b, GLA_DV, GLA_HEADS, GLA_DK), (0, 2, 3, 1))


def _sb_weights(z, tail_mat_ref, carry, mask):
    ls = _log_sigmoid(z)
    l1m = ls - z
    if mask is not None:
        l1m = jnp.where(mask, l1m, 0.0)
    hi, lo = _split2(l1m)
    cs = _dot(jnp.concatenate([hi, lo], axis=1), tail_mat_ref[...])
    w = jnp.exp(ls + cs[:, :LANES] + carry)
    if mask is not None:
        w = jnp.where(mask, w, 0.0)
    return w, carry + cs[:, LANES:]


def _tail_matrix():
    src = jnp.arange(2 * LANES)[:, None] % LANES
    dst = jnp.arange(2 * LANES)[None, :]
    return jnp.where((dst >= LANES) | (src > dst), 1.0, 0.0).astype(BF16)


def _sb_prompt_body(bias_ref, q_ref, k_ref, v_ref, tm_ref, o_ref, kbf_scr, vbd_scr, qbd_scr, carry_scr, acc_scr,
                    *, T):
    i = pl.program_id(1)
    nkb = T // SB_TILE
    R = SB_HEADS * SB_TILE
    lane_head = lax.broadcasted_iota(jnp.int32, (SB_TILE, SB_W), 1) // SB_DIM

    @pl.when(i == 0)
    def _():
        kbf_scr[...] = k_ref[0].astype(BF16)
        for jb in range(nkb):
            vb = v_ref[0, jb * SB_TILE:(jb + 1) * SB_TILE, :]
            for h in range(SB_HEADS):
                vbd_scr[jb, h * SB_TILE:(h + 1) * SB_TILE, :] = jnp.where(lane_head == h, vb, 0.0).astype(BF16)

    q = q_ref[0] * (SB_DIM ** -0.5)
    for h in range(SB_HEADS):
        qbd_scr[h * SB_TILE:(h + 1) * SB_TILE, :] = jnp.where(lane_head == h, q, 0.0).astype(BF16)
    carry_scr[...] = jnp.zeros_like(carry_scr)
    acc_scr[...] = jnp.zeros_like(acc_scr)

    def tile(j, mask):
        kblk = kbf_scr[pl.ds(pl.multiple_of(j * SB_TILE, SB_TILE), SB_TILE), :]
        z = _dot_nt(qbd_scr[...], kblk) + bias_ref[...]
        w, carry_scr[...] = _sb_weights(z, tm_ref, carry_scr[...], mask)
        wb = w.astype(BF16)
        wcat = jnp.concatenate([wb[h * SB_TILE:(h + 1) * SB_TILE] for h in range(SB_HEADS)], axis=1)
        acc_scr[...] += _dot(wcat, vbd_scr[j])

    trow = lax.broadcasted_iota(jnp.int32, (R, SB_TILE), 0) % SB_TILE
    scol = lax.broadcasted_iota(jnp.int32, (R, SB_TILE), 1)
    tile(i, scol < trow)

    def pair(jj, _):
        tile(i - 1 - 2 * jj, None)
        tile(i - 2 - 2 * jj, None)
        return 0

    lax.fori_loop(0, i // 2, pair, 0)

    @pl.when(i % 2 == 1)
    def _():
        tile(0, None)

    o_ref[0] = acc_scr[...]


def _sb_prompt(q, k, v, bias_rows, tail_mat):
    bsz, t, w = q.shape
    nq = t // SB_TILE
    full = pl.BlockSpec((1, t, w), lambda b, i: (b, 0, 0))
    qt = pl.BlockSpec((1, SB_TILE, w), lambda b, i: (b, i, 0))
    const = lambda a: pl.BlockSpec(a.shape, lambda b, i: (0, 0))
    return pl.pallas_call(
        functools.partial(_sb_prompt_body, T=t),
        grid=(bsz, nq),
        in_specs=[const(bias_rows), qt, full, full, const(tail_mat)],
        out_specs=qt,
        out_shape=jax.ShapeDtypeStruct((bsz, t, w), F32),
        scratch_shapes=[pltpu.VMEM((t, w), BF16), pltpu.VMEM((nq, SB_HEADS * SB_TILE, w), BF16),
                        pltpu.VMEM((SB_HEADS * SB_TILE, w), BF16), pltpu.VMEM((SB_HEADS * SB_TILE, LANES), F32),
                        pltpu.VMEM((SB_TILE, w), F32)],
        compiler_params=_cparams(("parallel", "arbitrary")),
        name="sb_prompt",
    )(bias_rows, q, k, v, tail_mat)


def _sb_decode_body(pt_ref, bias_ref, q_ref, kn_ref, vn_ref, tm_ref, kc_hbm, vc_hbm, o_ref,
                    kbuf, vbuf, sem, knew_scr, vnew_scr, carry_scr, acc_scr, *, TQ, NP, G, NB, NSLOT, LAYER):
    b = pl.program_id(0)
    nb = NB
    ng = NP // G
    R = SB_HEADS * TQ

    def copies(bb, grp, slot):
        out = []
        for p in range(G):
            page = pt_ref[bb, grp * G + p]
            keys = pl.ds(p * SB_TILE, SB_TILE)
            out.append(pltpu.make_async_copy(kc_hbm.at[LAYER, page], kbuf.at[slot, :, keys], sem.at[0, slot]))
            out.append(pltpu.make_async_copy(vc_hbm.at[LAYER, page], vbuf.at[slot, :, keys], sem.at[1, slot]))
        return out

    def start_group(step):
        for cp in copies(step // ng, ng - 1 - step % ng, step % NSLOT):
            cp.start()

    @pl.when(b == 0)
    def _():
        for s in range(min(NSLOT - 1, nb * ng)):
            start_group(s)

    lane_head = lax.broadcasted_iota(jnp.int32, (TQ, SB_W), 1) // SB_DIM
    q = q_ref[0] * (SB_DIM ** -0.5)
    qbd = jnp.concatenate([jnp.where(lane_head == h, q, 0.0) for h in range(SB_HEADS)], axis=0).astype(BF16)
    bias = jnp.concatenate([jnp.full((TQ, LANES), bias_ref[h], F32) for h in range(SB_HEADS)], axis=0)

    knew_scr[...] = jnp.zeros_like(knew_scr)
    vnew_scr[...] = jnp.zeros_like(vnew_scr)
    knew_scr[0:TQ, :] = kn_ref[0]
    vnew_scr[0:TQ, :] = vn_ref[0]
    trow = lax.broadcasted_iota(jnp.int32, (R, LANES), 0) % TQ
    scol = lax.broadcasted_iota(jnp.int32, (R, LANES), 1)
    z = _dot_nt(qbd, knew_scr[...].astype(BF16)) + bias
    w, carry = _sb_weights(z, tm_ref, jnp.zeros((R, LANES), F32), scol < trow)
    carry_scr[...] = carry
    acc_scr[...] = _dot(w.astype(BF16), vnew_scr[...].astype(BF16))

    def group(r, _):
        step = b * ng + r
        slot = step % NSLOT
        for cp in copies(b, ng - 1 - r, slot):
            cp.wait()

        @pl.when(step + NSLOT - 1 < nb * ng)
        def _():
            start_group(step + NSLOT - 1)

        kg = kbuf[slot].astype(BF16)
        vg = vbuf[slot].astype(BF16)
        zg = _dot(qbd, kg)
        zrows = jnp.concatenate([zg[:, p * LANES:(p + 1) * LANES] + bias for p in range(G)], axis=0)
        ls = _log_sigmoid(zrows)
        hi, lo = _split2(ls - zrows)
        cs = _dot(jnp.concatenate([hi, lo], axis=1), tm_ref[...])
        carry = carry_scr[...]
        ws = [None] * G
        for p in reversed(range(G)):
            rs = slice(p * R, (p + 1) * R)
            ws[p] = jnp.exp(ls[rs] + cs[rs, :LANES] + carry).astype(BF16)
            carry = carry + cs[rs, LANES:]
        carry_scr[...] = carry
        acc_scr[...] += _dot_nt(jnp.concatenate(ws, axis=1), vg)
        return 0

    lax.fori_loop(0, ng, group, 0)
    acc = acc_scr[...]
    out = jnp.zeros((TQ, SB_W), F32)
    for h in range(SB_HEADS):
        out = out + jnp.where(lane_head == h, acc[h * TQ:(h + 1) * TQ, :], 0.0)
    o_ref[0] = out


def _sb_decode(q, k_new, v_new, bias, tail_mat, cache_k, cache_v, page_table, layer):
    db, tq, w = q.shape
    npages = page_table.shape[1]
    g = math.gcd(npages, PAGES_PER_GROUP)
    assert cache_k.shape[2:] == (w, SB_TILE)
    tok = pl.BlockSpec((1, tq, w), lambda b, pt: (b, 0, 0))
    grid_spec = pltpu.PrefetchScalarGridSpec(
        num_scalar_prefetch=1,
        grid=(db,),
        in_specs=[pl.BlockSpec(memory_space=pltpu.SMEM), tok, tok, tok,
                  pl.BlockSpec(tail_mat.shape, lambda b, pt: (0, 0)),
                  pl.BlockSpec(memory_space=pl.ANY), pl.BlockSpec(memory_space=pl.ANY)],
        out_specs=tok,
        scratch_shapes=[pltpu.VMEM((DECODE_SLOTS, w, g * SB_TILE), F32), pltpu.VMEM((DECODE_SLOTS, w, g * SB_TILE), F32),
                        pltpu.SemaphoreType.DMA((2, DECODE_SLOTS)),
                        pltpu.VMEM((SB_TILE, w), F32), pltpu.VMEM((SB_TILE, w), F32),
                        pltpu.VMEM((SB_HEADS * tq, LANES), F32), pltpu.VMEM((SB_HEADS * tq, w), F32)],
    )
    return pl.pallas_call(
        functools.partial(_sb_decode_body, TQ=tq, NP=npages, G=g, NB=db, NSLOT=DECODE_SLOTS,
                          LAYER=layer),
        grid_spec=grid_spec,
        out_shape=jax.ShapeDtypeStruct((db, tq, w), F32),
        compiler_params=_cparams(("arbitrary",)),
        name="sb_decode",
    )(page_table, bias, q, k_new, v_new, tail_mat, cache_k, cache_v)


def _conv_body(u_ref, prev_ref, buf_ref, w_ref, cb_ref, lg_ref, lb_ref, y_ref, bufo_ref, win_scr, *, RT, NT):
    i = pl.program_id(1)
    hist = CONV_WIDTH - 1
    first = HALO - hist

    @pl.when(i == 0)
    def _():
        win_scr[first:HALO, :] = buf_ref[0]

    if NT > 1:
        @pl.when(i > 0)
        def _():
            win_scr[0:HALO, :] = prev_ref[0]

    win_scr[HALO:HALO + RT, :] = u_ref[0]
    st = min(RT, CONV_SUBTILE)
    n = HALO + st
    for base in range(0, RT, st):
        win = win_scr[base:base + n, :]
        acc = jnp.zeros((st, CONV_CH), F32)
        for r in range(SUBLANES):
            offs = [o for o in range(first, first + CONV_WIDTH) if o % SUBLANES == r]
            shifted = win if r == 0 else pltpu.roll(win, n - r, 0)
            for o in offs:
                acc = acc + shifted[o - r:o - r + st, :] * w_ref[o - first:o - first + 1, :]
        y = acc + cb_ref[...]
        mu = jnp.mean(y, axis=-1, keepdims=True)
        yc = y - mu
        yn = yc * lax.rsqrt(jnp.mean(yc * yc, axis=-1, keepdims=True) + EPS) * lg_ref[...] + lb_ref[...]
        y_ref[0, base:base + st, :] = yn * _sigmoid(yn)

    @pl.when(i == pl.num_programs(1) - 1)
    def _():
        bufo_ref[0] = win_scr[first + RT:HALO + RT, :]


def _conv(u, buf, w, cb, lg, lb):
    bsz, t, ch = u.shape
    hist = CONV_WIDTH - 1
    rt = min(t, CONV_TILE)
    nt = t // rt
    assert t % rt == 0 and (nt == 1 or rt % HALO == 0)
    per = rt // HALO if nt > 1 else 1
    tile = pl.BlockSpec((1, rt, ch), lambda b, i: (b, i, 0))
    prev_rows = HALO if nt > 1 else rt
    prev = pl.BlockSpec((1, prev_rows, ch), lambda b, i: (b, jnp.maximum(i * per - 1, 0), 0))
    state = pl.BlockSpec((1, hist, ch), lambda b, i: (b, 0, 0))
    vec = lambda r: pl.BlockSpec((r, ch), lambda b, i: (0, 0))
    return pl.pallas_call(
        functools.partial(_conv_body, RT=rt, NT=nt),
        grid=(bsz, nt),
        in_specs=[tile, prev, state, vec(CONV_WIDTH), vec(1), vec(1), vec(1)],
        out_specs=[tile, state],
        out_shape=[jax.ShapeDtypeStruct((bsz, t, ch), F32), jax.ShapeDtypeStruct((bsz, hist, ch), F32)],
        scratch_shapes=[pltpu.VMEM((HALO + rt, ch), F32)],
        compiler_params=_cparams(("parallel", "arbitrary")),
        name="conv_branch",
    )(u, u, buf, w, cb, lg, lb)


def _mixout_body(x_ref, og_ref, osb_ref, y_ref, w_ref, g_ref, o_ref):
    m = _dot(og_ref[...].astype(BF16), w_ref[0:GLA_V, :])
    m = m + _dot(osb_ref[...].astype(BF16), w_ref[GLA_V:GLA_V + SB_W, :])
    m = m + _dot(y_ref[...].astype(BF16), w_ref[GLA_V + SB_W:, :])
    o_ref[...] = x_ref[...] + _rms(m, g_ref[...])


def _mixout(x, og, osb, y, w_out, g, *, tm):
    n, d = x.shape
    tok = lambda w: pl.BlockSpec((tm, w), lambda i: (i, 0))
    return pl.pallas_call(
        _mixout_body,
        grid=(n // tm,),
        in_specs=[tok(d), tok(GLA_V), tok(SB_W), tok(CONV_CH),
                  pl.BlockSpec(w_out.shape, lambda i: (0, 0)), pl.BlockSpec((1, d), lambda i: (0, 0))],
        out_specs=tok(d),
        out_shape=jax.ShapeDtypeStruct((n, d), F32),
        compiler_params=_cparams(("parallel",)),
        name="mixer_out",
    )(x, og, osb, y, w_out, g)


def _layer_weights(l, w):
    d = w["w_in"].shape[1]
    w_in = w["w_in"][l]
    sizes = (GLA_QK, GLA_QK, GLA_V, GLA_V, GLA_RANK, SB_W, SB_W, SB_W, 2 * CONV_CH)
    offs = [0]
    for s in sizes:
        offs.append(offs[-1] + s)
    main = jnp.concatenate([w_in[:, :offs[4]], w_in[:, offs[5]:]], axis=1).astype(BF16)
    w_lr = jnp.zeros((d, LANES), F32).at[:, :GLA_RANK].set(w_in[:, offs[4]:offs[5]]).astype(BF16)
    w_a2 = jnp.zeros((LANES, GLA_QK), F32).at[:GLA_RANK, :].set(w["gla_w_a2"][l]).astype(BF16)
    row = lambda name: w[name][l][None, :]
    return dict(
        ffn1=(row("ffn1_norm_pre"), row("ffn1_norm_post"), w["ffn1_w_gate"][l].astype(BF16),
              w["ffn1_w_up"][l].astype(BF16), w["ffn1_w_down"][l].astype(BF16)),
        ffn2=(row("ffn2_norm_pre"), row("ffn2_norm_post"), w["ffn2_w_gate"][l].astype(BF16),
              w["ffn2_w_up"][l].astype(BF16), w["ffn2_w_down"][l].astype(BF16)),
        mixin=(row("mix_norm_pre"), main, w_lr, w_a2, row("gla_b_a")),
        gla_norm=row("gla_norm"),
        sb_bias=w["sb_bias"][l],
        sb_bias_rows=jnp.broadcast_to(jnp.repeat(w["sb_bias"][l], SB_TILE)[:, None], (SB_HEADS * SB_TILE, LANES)),
        conv=(w["conv_w"][l], row("conv_b"), row("conv_norm_g"), row("conv_norm_b")),
        w_out=w["w_out"][l].astype(BF16),
        mix_post=row("mix_norm_post"),
    )


def _token_tile(n):
    for tm in (1024, 512, 256, 128, 64, 32, 16, 8):
        if n % tm == 0:
            return tm
    raise ValueError(n)


def _ffn_tile(f):
    for tf in (256, 128):
        if f % tf == 0:
            return tf
    return f


def _group_layer(x, lw, s0, buf, tail_mat, attend):
    bsz, t, d = x.shape
    n = bsz * t
    tm = _token_tile(n)
    tf = _ffn_tile(lw["ffn1"][2].shape[1])
    x2 = _ffn(x.reshape(n, d), *lw["ffn1"], tm=tm, tf=tf)
    gq, gk, gv, gg, la, sq, sk, sv, u = _mixin(x2, *lw["mixin"], tm=min(tm, 512))
    seq = lambda a: a.reshape(bsz, t, a.shape[-1])
    o_gla, st = _gla(seq(gq), seq(gk), seq(gv), seq(la), seq(gg), _state_to_t(s0), lw["gla_norm"])
    o_sb = attend(seq(sq), seq(sk), seq(sv))
    y, buf_new = _conv(seq(u), buf, *lw["conv"])
    x3 = _mixout(x2, o_gla.reshape(n, GLA_V), o_sb.reshape(n, SB_W), y.reshape(n, CONV_CH), lw["w_out"],
                 lw["mix_post"], tm=min(tm, 512))
    x4 = _ffn(x3, *lw["ffn2"], tm=tm, tf=tf)
    heads = lambda a: a.reshape(bsz, t, SB_HEADS, SB_DIM)
    return x4.reshape(bsz, t, d), heads(sk), heads(sv), _state_from_t(st), buf_new


def kernel(x_prompt, x_sample, cache_sb_k, cache_sb_v, page_table, state_gla, state_conv, ffn1_norm_pre, ffn1_norm_post, ffn1_w_gate, ffn1_w_up, ffn1_w_down, mix_norm_pre, mix_norm_post, w_in, gla_w_a2, gla_b_a, gla_norm, sb_bias, conv_w, conv_b, conv_norm_g, conv_norm_b, w_out, ffn2_norm_pre, ffn2_norm_post, ffn2_w_gate, ffn2_w_up, ffn2_w_down):
    weights = dict(ffn1_norm_pre=ffn1_norm_pre, ffn1_norm_post=ffn1_norm_post, ffn1_w_gate=ffn1_w_gate,
                   ffn1_w_up=ffn1_w_up, ffn1_w_down=ffn1_w_down, mix_norm_pre=mix_norm_pre,
                   mix_norm_post=mix_norm_post, w_in=w_in, gla_w_a2=gla_w_a2, gla_b_a=gla_b_a, gla_norm=gla_norm,
                   sb_bias=sb_bias, conv_w=conv_w, conv_b=conv_b, conv_norm_g=conv_norm_g,
                   conv_norm_b=conv_norm_b, w_out=w_out, ffn2_norm_pre=ffn2_norm_pre,
                   ffn2_norm_post=ffn2_norm_post, ffn2_w_gate=ffn2_w_gate, ffn2_w_up=ffn2_w_up,
                   ffn2_w_down=ffn2_w_down)
    depth = w_in.shape[0]
    bsz = x_prompt.shape[0]
    n_pool, page = cache_sb_k.shape[1], cache_sb_k.shape[2]
    tail_mat = _tail_matrix()
    ck = jnp.transpose(cache_sb_k, (0, 1, 3, 4, 2)).reshape(depth, n_pool, SB_W, page)
    cv = jnp.transpose(cache_sb_v, (0, 1, 3, 4, 2)).reshape(depth, n_pool, SB_W, page)
    xp, xs = x_prompt, x_sample
    outs = [[] for _ in range(8)]
    for l in range(depth):
        lw = _layer_weights(l, weights)
        s0 = jnp.zeros((bsz, GLA_HEADS, GLA_DK, GLA_DV), F32)
        buf0 = jnp.zeros((bsz, CONV_WIDTH - 1, CONV_CH), F32)
        prompt_attend = lambda q, k, v: _sb_prompt(q, k, v, lw["sb_bias_rows"], tail_mat)
        xp, k1, v1, s1, c1 = _group_layer(xp, lw, s0, buf0, tail_mat, prompt_attend)
        decode_attend = lambda q, k, v, l=l: _sb_decode(q, k, v, lw["sb_bias"], tail_mat, ck, cv, page_table, l)
        xs, k2, v2, s2, c2 = _group_layer(xs, lw, state_gla[l], state_conv[l], tail_mat, decode_attend)
        for lst, val in zip(outs, (k1, v1, s1, c1, k2, v2, s2, c2)):
            lst.append(val)
    return (xp, xs) + tuple(jnp.stack(o) for o in outs)
```

```python
import functools
import math

import jax
import jax.numpy as jnp
from jax import lax
from jax.experimental import pallas as pl
from jax.experimental.pallas import tpu as pltpu

F32 = jnp.float32
BF16 = jnp.bfloat16

EPS = 1e-6
GLA_HEADS = 4
GLA_DK = 64
GLA_DV = 128
GLA_RANK = 16
GLA_TAU = 16.0
GLA_CHUNK = 128
SB_HEADS = 4
SB_DIM = 64
CONV_CH = 256
CONV_WIDTH = 31
GLA_QK = GLA_HEADS * GLA_DK
GLA_V = GLA_HEADS * GLA_DV
SB_W = SB_HEADS * SB_DIM

LANES = 128
SUBLANES = 8
SB_TILE = 128
HALO = 32
CONV_TILE = 256
CONV_SUBTILE = 64
GLA_SAFE_DECAY = 60.0
GLA_SEQS_PER_STEP = 4
PAGES_PER_GROUP = 16
DECODE_SLOTS = 4
VMEM_LIMIT = 48 * 1024 * 1024


def _cparams(sem):
    return pltpu.CompilerParams(dimension_semantics=sem, vmem_limit_bytes=VMEM_LIMIT)


def _rms(x, g):
    return x * lax.rsqrt(jnp.mean(x * x, axis=-1, keepdims=True) + EPS) * g


def _sigmoid(x):
    return 1.0 / (1.0 + jnp.exp(-x))


def _log_sigmoid(x):
    return jnp.minimum(x, 0.0) - jnp.log(1.0 + jnp.exp(-jnp.abs(x)))


def _split2(x):
    hi = x.astype(BF16)
    lo = (x - hi.astype(F32)).astype(BF16)
    return hi, lo


def _split3(x):
    hi = x.astype(BF16)
    r = x - hi.astype(F32)
    mid = r.astype(BF16)
    lo = (r - mid.astype(F32)).astype(BF16)
    return hi, mid, lo


def _dot(a, b):
    return jnp.dot(a, b, preferred_element_type=F32)


def _dot_nt(a, b):
    return lax.dot_general(a, b, (((1,), (1,)), ((), ())), preferred_element_type=F32)


def _dot_tn(a, b):
    return lax.dot_general(a, b, (((0,), (0,)), ((), ())), preferred_element_type=F32)


def _ffn_body(x_ref, gpre_ref, gpost_ref, wg_ref, wu_ref, wd_ref, o_ref, h_scr, acc_scr, *, TF):
    h_scr[...] = _rms(x_ref[...], gpre_ref[...]).astype(BF16)
    nf = wg_ref.shape[1] // TF

    def chunk(f):
        cols = pl.ds(pl.multiple_of(f * TF, TF), TF)
        h = h_scr[...]
        g = _dot(h, wg_ref[:, cols])
        u = _dot(h, wu_ref[:, cols])
        a = (g * _sigmoid(g)) * u
        return _dot(a.astype(BF16), wd_ref[cols, :])

    acc_scr[...] = chunk(0)

    def body(f, carry):
        acc_scr[...] += chunk(f)
        return carry

    lax.fori_loop(1, nf, body, 0)
    o_ref[...] = x_ref[...] + 0.5 * _rms(acc_scr[...], gpost_ref[...])


def _ffn(x, g_pre, g_post, w_gate, w_up, w_down, *, tm, tf):
    n, d = x.shape
    resident = lambda a: pl.BlockSpec(a.shape, lambda i: (0, 0), pipeline_mode=pl.Buffered(1))
    return pl.pallas_call(
        functools.partial(_ffn_body, TF=tf),
        grid=(n // tm,),
        in_specs=[
            pl.BlockSpec((tm, d), lambda i: (i, 0)),
            pl.BlockSpec((1, d), lambda i: (0, 0)),
            pl.BlockSpec((1, d), lambda i: (0, 0)),
            resident(w_gate), resident(w_up), resident(w_down),
        ],
        out_specs=pl.BlockSpec((tm, d), lambda i: (i, 0)),
        out_shape=jax.ShapeDtypeStruct((n, d), F32),
        scratch_shapes=[pltpu.VMEM((tm, d), BF16), pltpu.VMEM((tm, d), F32)],
        compiler_params=_cparams(("parallel",)),
        name="half_ffn",
    )(x, g_pre, g_post, w_gate, w_up, w_down)


_SEG = {}
_off = 0
for _name, _w in (("gq", GLA_QK), ("gk", GLA_QK), ("gv", GLA_V), ("gg", GLA_V), ("sq", SB_W), ("sk", SB_W),
                  ("sv", SB_W), ("ca", CONV_CH), ("cg", CONV_CH)):
    _SEG[_name] = (_off, _off + _w)
    _off += _w
MAIN_COLS = _off


def _mixin_body(x_ref, g_ref, w_ref, wlr_ref, wa2_ref, ba_ref, *rest, KV_T):
    if KV_T:
        wkvt_ref, rest = rest[0], rest[1:]
    gq_ref, gk_ref, gv_ref, gg_ref, la_ref, sq_ref, sk_ref, sv_ref, u_ref = rest
    h = _rms(x_ref[...], g_ref[...]).astype(BF16)

    def seg(name):
        lo, hi = _SEG[name]
        return _dot(h, w_ref[:, lo:hi])

    gq_ref[...] = seg("gq") * (GLA_DK ** -0.5)
    gk_ref[...] = seg("gk")
    gv_ref[...] = seg("gv")
    gg_ref[...] = seg("gg")
    sq_ref[...] = seg("sq")
    if KV_T:
        sk_ref[0] = _dot_nt(wkvt_ref[0:SB_W, :], h)
        sv_ref[0] = _dot_nt(wkvt_ref[SB_W:2 * SB_W, :], h)
    else:
        sk_ref[...] = seg("sk")
        sv_ref[...] = seg("sv")
    u_ref[...] = seg("ca") * _sigmoid(seg("cg"))
    lr = _dot(h, wlr_ref[...])
    xa = _dot(lr.astype(BF16), wa2_ref[...]) + ba_ref[...]
    la_ref[...] = _log_sigmoid(xa) * (1.0 / GLA_TAU)


def _mixin(x, g, w_main, w_lr, w_a2, b_a, w_kvt, *, tm, seq_len, kv_t):
    n, d = x.shape
    widths = (GLA_QK, GLA_QK, GLA_V, GLA_V, GLA_QK, SB_W, SB_W, SB_W, CONV_CH)
    const = lambda shape: pl.BlockSpec(shape, lambda i: (0, 0))
    in_specs = [pl.BlockSpec((tm, d), lambda i: (i, 0)), const((1, d)), const(w_main.shape),
                const(w_lr.shape), const(w_a2.shape), const((1, GLA_QK))]
    out_specs = [pl.BlockSpec((tm, w), lambda i: (i, 0)) for w in widths]
    out_shape = [jax.ShapeDtypeStruct((n, w), F32) for w in widths]
    args = [x, g, w_main, w_lr, w_a2, b_a]
    if kv_t:
        assert seq_len % tm == 0
        per_seq = seq_len // tm
        in_specs.append(const(w_kvt.shape))
        args.append(w_kvt)
        for idx in (6, 7):
            out_specs[idx] = pl.BlockSpec((1, SB_W, tm), lambda i: (i // per_seq, 0, i % per_seq))
            out_shape[idx] = jax.ShapeDtypeStruct((n // seq_len, SB_W, seq_len), F32)
    return pl.pallas_call(
        functools.partial(_mixin_body, KV_T=kv_t),
        grid=(n // tm,),
        in_specs=in_specs,
        out_specs=out_specs,
        out_shape=out_shape,
        compiler_params=_cparams(("parallel",)),
        name="mixer_in",
    )(*args)


def _gla_body(q_ref, k_ref, v_ref, la_ref, gg_ref, s0_ref, gn_ref, o_ref, sout_ref, st_scr, b_scr, oi_scr,
              *, C, NB):
    c = pl.program_id(1)

    @pl.when(c == 0)
    def _():
        st_scr[...] = s0_ref[...]

    row = lax.broadcasted_iota(jnp.int32, (C, C), 0)
    col = lax.broadcasted_iota(jnp.int32, (C, C), 1)
    causal = row >= col
    tri = jnp.where(causal, 1.0, 0.0).astype(BF16)
    lane_head = lax.broadcasted_iota(jnp.int32, (C, GLA_QK), 1) // GLA_DK
    heads = [slice(h * GLA_DV, (h + 1) * GLA_DV) for h in range(GLA_HEADS)]

    def head_rows(x, h):
        return jnp.where(lane_head == h, x, 0.0).astype(BF16)

    decays = []
    for n in range(NB):
        q, k, v = q_ref[n], k_ref[n], v_ref[n]
        hi, mid, lo = _split3(la_ref[n])
        b = _dot(tri, jnp.concatenate([hi, mid, lo], axis=1))
        b = b[:, :GLA_QK] + b[:, GLA_QK:2 * GLA_QK] + b[:, 2 * GLA_QK:]
        b_scr[n] = b
        b_last = b[C - 1:C, :]
        decays.append(jnp.max(-b_last))
        st = st_scr[n]
        qb = q * jnp.exp(b)
        kd = k * jnp.exp(b_last - b)
        for h, sl in enumerate(heads):
            oi_scr[n, :, sl] = _dot_nt(head_rows(qb, h), st.astype(BF16))
        upd = jnp.zeros_like(st)
        for h, sl in enumerate(heads):
            upd = upd + _dot_tn(v[:, sl].astype(BF16), head_rows(kd, h))
        st_scr[n] = st * jnp.exp(b_last) + upd
    total_decay = functools.reduce(jnp.maximum, decays)

    @pl.when(total_decay <= GLA_SAFE_DECAY)
    def _():
        for n in range(NB):
            q, k, v, b = q_ref[n], k_ref[n], v_ref[n], b_scr[n]
            qb = q * jnp.exp(b)
            kb = (k * jnp.exp(-b)).astype(BF16)
            for h, sl in enumerate(heads):
                attn = jnp.where(causal, _dot_nt(head_rows(qb, h), kb), 0.0)
                oi_scr[n, :, sl] += _dot(attn.astype(BF16), v[:, sl].astype(BF16))

    @pl.when(total_decay > GLA_SAFE_DECAY)
    def _():
        rhead = lax.broadcasted_iota(jnp.int32, (GLA_QK, GLA_V), 0) // GLA_DK
        chead = lax.broadcasted_iota(jnp.int32, (GLA_QK, GLA_V), 1) // GLA_DV
        spread = jnp.where(rhead == chead, 1.0, 0.0).astype(BF16)
        trow = lax.broadcasted_iota(jnp.int32, (C, GLA_QK), 0)
        for n in range(NB):
            q, b = q_ref[n], b_scr[n]

            def body(s, acc):
                bs = b_scr[n, pl.ds(s, 1), :]
                ks = k_ref[n, pl.ds(s, 1), :]
                vs = v_ref[n, pl.ds(s, 1), :]
                p = q * ks * jnp.exp(jnp.minimum(b - bs, 0.0))
                p = jnp.where(trow >= s, p, 0.0)
                p_hi, p_mid, p_lo = _split3(p)
                r = _dot(p_hi, spread) + _dot(p_mid, spread) + _dot(p_lo, spread)
                return acc + r * vs

            oi_scr[n] += lax.fori_loop(0, C, body, jnp.zeros((C, GLA_V), F32))

    for n in range(NB):
        o, gg = oi_scr[n], gg_ref[n]
        for sl in heads:
            o_ref[n, :, sl] = _rms(o[:, sl], gn_ref[:, sl]) * (gg[:, sl] * _sigmoid(gg[:, sl]))

    @pl.when(c == pl.num_programs(1) - 1)
    def _():
        sout_ref[...] = st_scr[...]


def _gla(q, k, v, la, gg, s0_t, gnorm):
    bsz, t_real, _ = q.shape
    if t_real < 16:
        padt = lambda a: jnp.pad(a, ((0, 0), (0, 16 - t_real), (0, 0)))
        q, k, v, la, gg = (padt(a) for a in (q, k, v, la, gg))
    t = q.shape[1]
    c = min(GLA_CHUNK, t)
    nb = GLA_SEQS_PER_STEP if bsz % GLA_SEQS_PER_STEP == 0 else 1
    assert t % c == 0
    tok = lambda w: pl.BlockSpec((nb, c, w), lambda b, i: (b, i, 0))
    st = pl.BlockSpec((nb, GLA_DV, GLA_QK), lambda b, i: (b, 0, 0))
    o, st_new = pl.pallas_call(
        functools.partial(_gla_body, C=c, NB=nb),
        grid=(bsz // nb, t // c),
        in_specs=[tok(GLA_QK), tok(GLA_QK), tok(GLA_V), tok(GLA_QK), tok(GLA_V), st,
                  pl.BlockSpec((1, GLA_V), lambda b, i: (0, 0))],
        out_specs=[tok(GLA_V), st],
        out_shape=[jax.ShapeDtypeStruct((bsz, t, GLA_V), F32), jax.ShapeDtypeStruct((bsz, GLA_DV, GLA_QK), F32)],
        scratch_shapes=[pltpu.VMEM((nb, GLA_DV, GLA_QK), F32), pltpu.VMEM((nb, c, GLA_QK), F32),
                        pltpu.VMEM((nb, c, GLA_V), F32)],
        compiler_params=_cparams(("parallel", "arbitrary")),
        name="gla",
    )(q, k, v, la, gg, s0_t, gnorm)
    return o[:, :t_real], st_new


def _state_to_t(s):
    b = s.shape[0]
    return jnp.transpose(s, (0, 3, 1, 2)).reshape(b, GLA_DV, GLA_QK)


def _state_from_t(st):
    b = st.shape[0]
    return jnp.transpose(st.reshape(b, GLA_DV, GLA_HEADS, GLA_DK), (0, 2, 3, 1))


def _sb_weights(z, tail_mat_ref, carry, mask):
    ls = _log_sigmoid(z)
    l1m = ls - z
    if mask is not None:
        l1m = jnp.where(mask, l1m, 0.0)
    hi, lo = _split2(l1m)
    cs = _dot(jnp.concatenate([hi, lo], axis=1), tail_mat_ref[...])
    w = jnp.exp(ls + cs[:, :LANES] + carry)
    if mask is not None:
        w = jnp.where(mask, w, 0.0)
    return w, carry + cs[:, LANES:]


def _tail_matrix():
    src = jnp.arange(2 * LANES)[:, None] % LANES
    dst = jnp.arange(2 * LANES)[None, :]
    return jnp.where((dst >= LANES) | (src > dst), 1.0, 0.0).astype(BF16)


def _sb_prompt_body(bias_ref, q_ref, k_ref, v_ref, tm_ref, o_ref, kbf_scr, vbd_scr, qbd_scr, carry_scr, acc_scr,
                    *, T):
    i = pl.program_id(1)
    nkb = T // SB_TILE
    R = SB_HEADS * SB_TILE
    lane_head = lax.broadcasted_iota(jnp.int32, (SB_TILE, SB_W), 1) // SB_DIM

    @pl.when(i == 0)
    def _():
        kbf_scr[...] = k_ref[0].astype(BF16)
        for jb in range(nkb):
            vb = v_ref[0, :, jb * SB_TILE:(jb + 1) * SB_TILE].T
            for h in range(SB_HEADS):
                vbd_scr[jb, h * SB_TILE:(h + 1) * SB_TILE, :] = jnp.where(lane_head == h, vb, 0.0).astype(BF16)

    q = q_ref[0] * (SB_DIM ** -0.5)
    for h in range(SB_HEADS):
        qbd_scr[h * SB_TILE:(h + 1) * SB_TILE, :] = jnp.where(lane_head == h, q, 0.0).astype(BF16)
    carry_scr[...] = jnp.zeros_like(carry_scr)
    acc_scr[...] = jnp.zeros_like(acc_scr)

    def tile(j, mask):
        kblk = kbf_scr[:, pl.ds(pl.multiple_of(j * SB_TILE, SB_TILE), SB_TILE)]
        z = _dot(qbd_scr[...], kblk) + bias_ref[...]
        w, carry_scr[...] = _sb_weights(z, tm_ref, carry_scr[...], mask)
        wb = w.astype(BF16)
        wcat = jnp.concatenate([wb[h * SB_TILE:(h + 1) * SB_TILE] for h in range(SB_HEADS)], axis=1)
        acc_scr[...] += _dot(wcat, vbd_scr[j])

    trow = lax.broadcasted_iota(jnp.int32, (R, SB_TILE), 0) % SB_TILE
    scol = lax.broadcasted_iota(jnp.int32, (R, SB_TILE), 1)
    tile(i, scol < trow)

    def quad(jj, _):
        for u in range(4):
            tile(i - 1 - u - 4 * jj, None)
        return 0

    lax.fori_loop(0, i // 4, quad, 0)
    rem = i % 4

    @pl.when(rem >= 2)
    def _():
        tile(rem - 1, None)
        tile(rem - 2, None)

    @pl.when(rem % 2 == 1)
    def _():
        tile(0, None)

    o_ref[0] = acc_scr[...]


def _sb_prompt(q, k_t, v_t, bias_rows, tail_mat):
    bsz, t, w = q.shape
    nq = t // SB_TILE
    full = pl.BlockSpec((1, w, t), lambda b, i: (b, 0, 0))
    qt = pl.BlockSpec((1, SB_TILE, w), lambda b, i: (b, i, 0))
    const = lambda a: pl.BlockSpec(a.shape, lambda b, i: (0, 0))
    return pl.pallas_call(
        functools.partial(_sb_prompt_body, T=t),
        grid=(bsz, nq),
        in_specs=[const(bias_rows), qt, full, full, const(tail_mat)],
        out_specs=qt,
        out_shape=jax.ShapeDtypeStruct((bsz, t, w), F32),
        scratch_shapes=[pltpu.VMEM((w, t), BF16), pltpu.VMEM((nq, SB_HEADS * SB_TILE, w), BF16),
                        pltpu.VMEM((SB_HEADS * SB_TILE, w), BF16), pltpu.VMEM((SB_HEADS * SB_TILE, LANES), F32),
                        pltpu.VMEM((SB_TILE, w), F32)],
        compiler_params=_cparams(("parallel", "arbitrary")),
        name="sb_prompt",
    )(bias_rows, q, k_t, v_t, tail_mat)


def _sb_decode_body(pt_ref, bias_ref, q_ref, kn_ref, vn_ref, tm_ref, kc_hbm, vc_hbm, o_ref,
                    kbuf, vbuf, sem, knew_scr, vnew_scr, carry_scr, acc_scr, *, TQ, NP, G, NB, NSLOT, LAYER):
    b = pl.program_id(0)
    nb = NB
    ng = NP // G
    R = SB_HEADS * TQ

    def copies(bb, grp, slot):
        out = []
        for p in range(G):
            page = pt_ref[bb, grp * G + p]
            keys = pl.ds(p * SB_TILE, SB_TILE)
            out.append(pltpu.make_async_copy(kc_hbm.at[LAYER, page], kbuf.at[slot, :, keys], sem.at[0, slot]))
            out.append(pltpu.make_async_copy(vc_hbm.at[LAYER, page], vbuf.at[slot, :, keys], sem.at[1, slot]))
        return out

    def start_group(step):
        for cp in copies(step // ng, ng - 1 - step % ng, step % NSLOT):
            cp.start()

    @pl.when(b == 0)
    def _():
        for s in range(min(NSLOT - 1, nb * ng)):
            start_group(s)

    lane_head = lax.broadcasted_iota(jnp.int32, (TQ, SB_W), 1) // SB_DIM
    q = q_ref[0] * (SB_DIM ** -0.5)
    qbd = jnp.concatenate([jnp.where(lane_head == h, q, 0.0) for h in range(SB_HEADS)], axis=0).astype(BF16)
    bias = jnp.concatenate([jnp.full((TQ, LANES), bias_ref[h], F32) for h in range(SB_HEADS)], axis=0)

    knew_scr[...] = jnp.zeros_like(knew_scr)
    vnew_scr[...] = jnp.zeros_like(vnew_scr)
    knew_scr[0:TQ, :] = kn_ref[0]
    vnew_scr[0:TQ, :] = vn_ref[0]
    trow = lax.broadcasted_iota(jnp.int32, (R, LANES), 0) % TQ
    scol = lax.broadcasted_iota(jnp.int32, (R, LANES), 1)
    z = _dot_nt(qbd, knew_scr[...].astype(BF16)) + bias
    w, carry = _sb_weights(z, tm_ref, jnp.zeros((R, LANES), F32), scol < trow)
    carry_scr[...] = carry
    acc_scr[...] = _dot(w.astype(BF16), vnew_scr[...].astype(BF16))

    def group(r, _):
        step = b * ng + r
        slot = step % NSLOT
        for cp in copies(b, ng - 1 - r, slot):
            cp.wait()

        @pl.when(step + NSLOT - 1 < nb * ng)
        def _():
            start_group(step + NSLOT - 1)

        kg = kbuf[slot].astype(BF16)
        vg = vbuf[slot].astype(BF16)
        zg = _dot(qbd, kg)
        zrows = jnp.concatenate([zg[:, p * LANES:(p + 1) * LANES] + bias for p in range(G)], axis=0)
        ls = _log_sigmoid(zrows)
        hi, lo = _split2(ls - zrows)
        cs = _dot(jnp.concatenate([hi, lo], axis=1), tm_ref[...])
        carry = carry_scr[...]
        ws = [None] * G
        for p in reversed(range(G)):
            rs = slice(p * R, (p + 1) * R)
            ws[p] = jnp.exp(ls[rs] + cs[rs, :LANES] + carry).astype(BF16)
            carry = carry + cs[rs, LANES:]
        carry_scr[...] = carry
        acc_scr[...] += _dot_nt(jnp.concatenate(ws, axis=1), vg)
        return 0

    lax.fori_loop(0, ng, group, 0)
    acc = acc_scr[...]
    out = jnp.zeros((TQ, SB_W), F32)
    for h in range(SB_HEADS):
        out = out + jnp.where(lane_head == h, acc[h * TQ:(h + 1) * TQ, :], 0.0)
    o_ref[0] = out


def _sb_decode(q, k_new, v_new, bias, tail_mat, cache_k, cache_v, page_table, layer):
    db, tq, w = q.shape
    npages = page_table.shape[1]
    g = math.gcd(npages, PAGES_PER_GROUP)
    assert cache_k.shape[2:] == (w, SB_TILE)
    tok = pl.BlockSpec((1, tq, w), lambda b, pt: (b, 0, 0))
    grid_spec = pltpu.PrefetchScalarGridSpec(
        num_scalar_prefetch=1,
        grid=(db,),
        in_specs=[pl.BlockSpec(memory_space=pltpu.SMEM), tok, tok, tok,
                  pl.BlockSpec(tail_mat.shape, lambda b, pt: (0, 0)),
                  pl.BlockSpec(memory_space=pl.ANY), pl.BlockSpec(memory_space=pl.ANY)],
        out_specs=tok,
        scratch_shapes=[pltpu.VMEM((DECODE_SLOTS, w, g * SB_TILE), F32), pltpu.VMEM((DECODE_SLOTS, w, g * SB_TILE), F32),
                        pltpu.SemaphoreType.DMA((2, DECODE_SLOTS)),
                        pltpu.VMEM((SB_TILE, w), F32), pltpu.VMEM((SB_TILE, w), F32),
                        pltpu.VMEM((SB_HEADS * tq, LANES), F32), pltpu.VMEM((SB_HEADS * tq, w), F32)],
    )
    return pl.pallas_call(
        functools.partial(_sb_decode_body, TQ=tq, NP=npages, G=g, NB=db, NSLOT=DECODE_SLOTS,
                          LAYER=layer),
        grid_spec=grid_spec,
        out_shape=jax.ShapeDtypeStruct((db, tq, w), F32),
        compiler_params=_cparams(("arbitrary",)),
        name="sb_decode",
    )(page_table, bias, q, k_new, v_new, tail_mat, cache_k, cache_v)


def _conv_body(u_ref, prev_ref, buf_ref, w_ref, cb_ref, lg_ref, lb_ref, y_ref, bufo_ref, win_scr, *, RT, NT):
    i = pl.program_id(1)
    hist = CONV_WIDTH - 1
    first = HALO - hist

    @pl.when(i == 0)
    def _():
        win_scr[first:HALO, :] = buf_ref[0]

    if NT > 1:
        @pl.when(i > 0)
        def _():
            win_scr[0:HALO, :] = prev_ref[0]

    win_scr[HALO:HALO + RT, :] = u_ref[0]
    st = min(RT, CONV_SUBTILE)
    n = HALO + st
    for base in range(0, RT, st):
        win = win_scr[base:base + n, :]
        acc = jnp.zeros((st, CONV_CH), F32)
        for r in range(SUBLANES):
            offs = [o for o in range(first, first + CONV_WIDTH) if o % SUBLANES == r]
            shifted = win if r == 0 else pltpu.roll(win, n - r, 0)
            for o in offs:
                acc = acc + shifted[o - r:o - r + st, :] * w_ref[o - first:o - first + 1, :]
        y = acc + cb_ref[...]
        mu = jnp.mean(y, axis=-1, keepdims=True)
        yc = y - mu
        yn = yc * lax.rsqrt(jnp.mean(yc * yc, axis=-1, keepdims=True) + EPS) * lg_ref[...] + lb_ref[...]
        y_ref[0, base:base + st, :] = yn * _sigmoid(yn)

    @pl.when(i == pl.num_programs(1) - 1)
    def _():
        bufo_ref[0] = win_scr[first + RT:HALO + RT, :]


def _conv(u, buf, w, cb, lg, lb):
    bsz, t, ch = u.shape
    hist = CONV_WIDTH - 1
    rt = min(t, CONV_TILE)
    nt = t // rt
    assert t % rt == 0 and (nt == 1 or rt % HALO == 0)
    per = rt // HALO if nt > 1 else 1
    tile = pl.BlockSpec((1, rt, ch), lambda b, i: (b, i, 0))
    prev_rows = HALO if nt > 1 else rt
    prev = pl.BlockSpec((1, prev_rows, ch), lambda b, i: (b, jnp.maximum(i * per - 1, 0), 0))
    state = pl.BlockSpec((1, hist, ch), lambda b, i: (b, 0, 0))
    vec = lambda r: pl.BlockSpec((r, ch), lambda b, i: (0, 0))
    return pl.pallas_call(
        functools.partial(_conv_body, RT=rt, NT=nt),
        grid=(bsz, nt),
        in_specs=[tile, prev, state, vec(CONV_WIDTH), vec(1), vec(1), vec(1)],
        out_specs=[tile, state],
        out_shape=[jax.ShapeDtypeStruct((bsz, t, ch), F32), jax.ShapeDtypeStruct((bsz, hist, ch), F32)],
        scratch_shapes=[pltpu.VMEM((HALO + rt, ch), F32)],
        compiler_params=_cparams(("parallel", "arbitrary")),
        name="conv_branch",
    )(u, u, buf, w, cb, lg, lb)


def _mixout_body(x_ref, og_ref, osb_ref, y_ref, w_ref, g_ref, o_ref):
    m = _dot(og_ref[...].astype(BF16), w_ref[0:GLA_V, :])
    m = m + _dot(osb_ref[...].astype(BF16), w_ref[GLA_V:GLA_V + SB_W, :])
    m = m + _dot(y_ref[...].astype(BF16), w_ref[GLA_V + SB_W:, :])
    o_ref[...] = x_ref[...] + _rms(m, g_ref[...])


def _mixout(x, og, osb, y, w_out, g, *, tm):
    n, d = x.shape
    tok = lambda w: pl.BlockSpec((tm, w), lambda i: (i, 0))
    return pl.pallas_call(
        _mixout_body,
        grid=(n // tm,),
        in_specs=[tok(d), tok(GLA_V), tok(SB_W), tok(CONV_CH),
                  pl.BlockSpec(w_out.shape, lambda i: (0, 0)), pl.BlockSpec((1, d), lambda i: (0, 0))],
        out_specs=tok(d),
        out_shape=jax.ShapeDtypeStruct((n, d), F32),
        compiler_params=_cparams(("parallel",)),
        name="mixer_out",
    )(x, og, osb, y, w_out, g)


def _layer_weights(l, w):
    d = w["w_in"].shape[1]
    w_in = w["w_in"][l]
    sizes = (GLA_QK, GLA_QK, GLA_V, GLA_V, GLA_RANK, SB_W, SB_W, SB_W, 2 * CONV_CH)
    offs = [0]
    for s in sizes:
        offs.append(offs[-1] + s)
    main = jnp.concatenate([w_in[:, :offs[4]], w_in[:, offs[5]:]], axis=1).astype(BF16)
    w_lr = jnp.zeros((d, LANES), F32).at[:, :GLA_RANK].set(w_in[:, offs[4]:offs[5]]).astype(BF16)
    w_a2 = jnp.zeros((LANES, GLA_QK), F32).at[:GLA_RANK, :].set(w["gla_w_a2"][l]).astype(BF16)
    row = lambda name: w[name][l][None, :]
    return dict(
        ffn1=(row("ffn1_norm_pre"), row("ffn1_norm_post"), w["ffn1_w_gate"][l].astype(BF16),
              w["ffn1_w_up"][l].astype(BF16), w["ffn1_w_down"][l].astype(BF16)),
        ffn2=(row("ffn2_norm_pre"), row("ffn2_norm_post"), w["ffn2_w_gate"][l].astype(BF16),
              w["ffn2_w_up"][l].astype(BF16), w["ffn2_w_down"][l].astype(BF16)),
        mixin=(row("mix_norm_pre"), main, w_lr, w_a2, row("gla_b_a"), w_in[:, offs[6]:offs[8]].T.astype(BF16)),
        gla_norm=row("gla_norm"),
        sb_bias=w["sb_bias"][l],
        sb_bias_rows=jnp.broadcast_to(jnp.repeat(w["sb_bias"][l], SB_TILE)[:, None], (SB_HEADS * SB_TILE, LANES)),
        conv=(w["conv_w"][l], row("conv_b"), row("conv_norm_g"), row("conv_norm_b")),
        w_out=w["w_out"][l].astype(BF16),
        mix_post=row("mix_norm_post"),
    )


def _token_tile(n):
    for tm in (1024, 512, 256, 128, 64, 32, 16, 8):
        if n % tm == 0:
            return tm
    raise ValueError(n)


def _ffn_tile(f):
    for tf in (256, 128):
        if f % tf == 0:
            return tf
    return f


def _group_layer(x, lw, s0, buf, tail_mat, attend, kv_t):
    bsz, t, d = x.shape
    n = bsz * t
    tm = _token_tile(n)
    tf = _ffn_tile(lw["ffn1"][2].shape[1])
    x2 = _ffn(x.reshape(n, d), *lw["ffn1"], tm=tm, tf=tf)
    gq, gk, gv, gg, la, sq, sk, sv, u = _mixin(x2, *lw["mixin"], tm=min(tm, 512, t) if kv_t else min(tm, 512),
                                               seq_len=t, kv_t=kv_t)
    seq = lambda a: a.reshape(bsz, t, a.shape[-1])
    o_gla, st = _gla(seq(gq), seq(gk), seq(gv), seq(la), seq(gg), _state_to_t(s0), lw["gla_norm"])
    o_sb = attend(seq(sq), sk, sv) if kv_t else attend(seq(sq), seq(sk), seq(sv))
    y, buf_new = _conv(seq(u), buf, *lw["conv"])
    x3 = _mixout(x2, o_gla.reshape(n, GLA_V), o_sb.reshape(n, SB_W), y.reshape(n, CONV_CH), lw["w_out"],
                 lw["mix_post"], tm=min(tm, 512))
    x4 = _ffn(x3, *lw["ffn2"], tm=tm, tf=tf)
    if kv_t:
        heads = lambda a: a.reshape(bsz, SB_HEADS, SB_DIM, t)
    else:
        heads = lambda a: a.reshape(bsz, t, SB_HEADS, SB_DIM)
    return x4.reshape(bsz, t, d), heads(sk), heads(sv), _state_from_t(st), buf_new


def kernel(x_prompt, x_sample, cache_sb_k, cache_sb_v, page_table, state_gla, state_conv, ffn1_norm_pre, ffn1_norm_post, ffn1_w_gate, ffn1_w_up, ffn1_w_down, mix_norm_pre, mix_norm_post, w_in, gla_w_a2, gla_b_a, gla_norm, sb_bias, conv_w, conv_b, conv_norm_g, conv_norm_b, w_out, ffn2_norm_pre, ffn2_norm_post, ffn2_w_gate, ffn2_w_up, ffn2_w_down):
    weights = dict(ffn1_norm_pre=ffn1_norm_pre, ffn1_norm_post=ffn1_norm_post, ffn1_w_gate=ffn1_w_gate,
                   ffn1_w_up=ffn1_w_up, ffn1_w_down=ffn1_w_down, mix_norm_pre=mix_norm_pre,
                   mix_norm_post=mix_norm_post, w_in=w_in, gla_w_a2=gla_w_a2, gla_b_a=gla_b_a, gla_norm=gla_norm,
                   sb_bias=sb_bias, conv_w=conv_w, conv_b=conv_b, conv_norm_g=conv_norm_g,
                   conv_norm_b=conv_norm_b, w_out=w_out, ffn2_norm_pre=ffn2_norm_pre,
                   ffn2_norm_post=ffn2_norm_post, ffn2_w_gate=ffn2_w_gate, ffn2_w_up=ffn2_w_up,
                   ffn2_w_down=ffn2_w_down)
    depth = w_in.shape[0]
    bsz = x_prompt.shape[0]
    n_pool, page = cache_sb_k.shape[1], cache_sb_k.shape[2]
    tail_mat = _tail_matrix()
    ck = jnp.transpose(cache_sb_k, (0, 1, 3, 4, 2)).reshape(depth, n_pool, SB_W, page)
    cv = jnp.transpose(cache_sb_v, (0, 1, 3, 4, 2)).reshape(depth, n_pool, SB_W, page)
    xp, xs = x_prompt, x_sample
    outs = [[] for _ in range(8)]
    for l in range(depth):
        lw = _layer_weights(l, weights)
        s0 = jnp.zeros((bsz, GLA_HEADS, GLA_DK, GLA_DV), F32)
        buf0 = jnp.zeros((bsz, CONV_WIDTH - 1, CONV_CH), F32)
        prompt_attend = lambda q, k, v: _sb_prompt(q, k, v, lw["sb_bias_rows"], tail_mat)
        xp, k1, v1, s1, c1 = _group_layer(xp, lw, s0, buf0, tail_mat, prompt_attend, True)
        decode_attend = lambda q, k, v, l=l: _sb_decode(q, k, v, lw["sb_bias"], tail_mat, ck, cv, page_table, l)
        xs, k2, v2, s2, c2 = _group_layer(xs, lw, state_gla[l], state_conv[l], tail_mat, decode_attend, False)
        for lst, val in zip(outs, (k1, v1, s1, c1, k2, v2, s2, c2)):
            lst.append(val)
    stacked = [jnp.stack(o) for o in outs]
    for idx in (0, 1):
        stacked[idx] = jnp.transpose(stacked[idx], (0, 1, 4, 2, 3))
    return (xp, xs) + tuple(stacked)
```

```python
import functools
import math

import jax
import jax.numpy as jnp
from jax import lax
from jax.experimental import pallas as pl
from jax.experimental.pallas import tpu as pltpu

F32 = jnp.float32
BF16 = jnp.bfloat16

EPS = 1e-6
GLA_HEADS = 4
GLA_DK = 64
GLA_DV = 128
GLA_RANK = 16
GLA_TAU = 16.0
GLA_CHUNK = 128
SB_HEADS = 4
SB_DIM = 64
CONV_CH = 256
CONV_WIDTH = 31
GLA_QK = GLA_HEADS * GLA_DK
GLA_V = GLA_HEADS * GLA_DV
SB_W = SB_HEADS * SB_DIM

LANES = 128
SUBLANES = 8
SB_TILE = 128
SB_TILES_PER_TRIP = 4
HALO = 32
CONV_TILE = 256
CONV_SUBTILE = 64
GLA_SAFE_DECAY = 60.0
GLA_SEQS_PER_STEP = 4
PAGES_PER_GROUP = 16
DECODE_SLOTS = 4
VMEM_LIMIT = 48 * 1024 * 1024


def _cparams(sem):
    return pltpu.CompilerParams(dimension_semantics=sem, vmem_limit_bytes=VMEM_LIMIT)


def _rms(x, g):
    return x * lax.rsqrt(jnp.mean(x * x, axis=-1, keepdims=True) + EPS) * g


def _sigmoid(x):
    return 1.0 / (1.0 + jnp.exp(-x))


def _log_sigmoid(x):
    return jnp.minimum(x, 0.0) - jnp.log(1.0 + jnp.exp(-jnp.abs(x)))


def _split2(x):
    hi = x.astype(BF16)
    lo = (x - hi.astype(F32)).astype(BF16)
    return hi, lo


def _split3(x):
    hi = x.astype(BF16)
    r = x - hi.astype(F32)
    mid = r.astype(BF16)
    lo = (r - mid.astype(F32)).astype(BF16)
    return hi, mid, lo


def _dot(a, b):
    return jnp.dot(a, b, preferred_element_type=F32)


def _dot_nt(a, b):
    return lax.dot_general(a, b, (((1,), (1,)), ((), ())), preferred_element_type=F32)


def _dot_tn(a, b):
    return lax.dot_general(a, b, (((0,), (0,)), ((), ())), preferred_element_type=F32)


def _ffn_body(x_ref, gpre_ref, gpost_ref, wg_ref, wu_ref, wd_ref, o_ref, h_scr, acc_scr, *, TF):
    h_scr[...] = _rms(x_ref[...], gpre_ref[...]).astype(BF16)
    nf = wg_ref.shape[1] // TF

    def chunk(f):
        cols = pl.ds(pl.multiple_of(f * TF, TF), TF)
        h = h_scr[...]
        g = _dot(h, wg_ref[:, cols])
        u = _dot(h, wu_ref[:, cols])
        a = (g * _sigmoid(g)) * u
        return _dot(a.astype(BF16), wd_ref[cols, :])

    acc_scr[...] = chunk(0)

    def body(f, carry):
        acc_scr[...] += chunk(f)
        return carry

    lax.fori_loop(1, nf, body, 0)
    o_ref[...] = x_ref[...] + 0.5 * _rms(acc_scr[...], gpost_ref[...])


def _ffn(x, g_pre, g_post, w_gate, w_up, w_down, layer, *, tm, tf):
    n, d = x.shape
    resident = lambda a: pl.BlockSpec((None,) + a.shape[1:], lambda i: (layer, 0, 0), pipeline_mode=pl.Buffered(1))
    return pl.pallas_call(
        functools.partial(_ffn_body, TF=tf),
        grid=(n // tm,),
        in_specs=[
            pl.BlockSpec((tm, d), lambda i: (i, 0)),
            pl.BlockSpec((None, 1, d), lambda i: (layer, 0, 0)),
            pl.BlockSpec((None, 1, d), lambda i: (layer, 0, 0)),
            resident(w_gate), resident(w_up), resident(w_down),
        ],
        out_specs=pl.BlockSpec((tm, d), lambda i: (i, 0)),
        out_shape=jax.ShapeDtypeStruct((n, d), F32),
        scratch_shapes=[pltpu.VMEM((tm, d), BF16), pltpu.VMEM((tm, d), F32)],
        compiler_params=_cparams(("parallel",)),
        name="half_ffn",
    )(x, g_pre, g_post, w_gate, w_up, w_down)


_SEG = {}
_off = 0
for _name, _w in (("gq", GLA_QK), ("gk", GLA_QK), ("gv", GLA_V), ("gg", GLA_V), ("sq", SB_W), ("sk", SB_W),
                  ("sv", SB_W), ("ca", CONV_CH), ("cg", CONV_CH)):
    _SEG[_name] = (_off, _off + _w)
    _off += _w
MAIN_COLS = _off


def _mixin_body(x_ref, g_ref, w_ref, wlr_ref, wa2_ref, ba_ref, *rest, KV_T):
    if KV_T:
        wkvt_ref, rest = rest[0], rest[1:]
    gq_ref, gk_ref, gv_ref, gg_ref, la_ref, sq_ref, sk_ref, sv_ref, u_ref = rest
    h = _rms(x_ref[...], g_ref[...]).astype(BF16)

    def seg(name):
        lo, hi = _SEG[name]
        return _dot(h, w_ref[:, lo:hi])

    gq_ref[...] = seg("gq") * (GLA_DK ** -0.5)
    gk_ref[...] = seg("gk")
    gv_ref[...] = seg("gv")
    gg_ref[...] = seg("gg")
    sq_ref[...] = seg("sq")
    if KV_T:
        sk_ref[0] = _dot_nt(wkvt_ref[0:SB_W, :], h)
        sv_ref[0] = _dot_nt(wkvt_ref[SB_W:2 * SB_W, :], h)
    else:
        sk_ref[...] = seg("sk")
        sv_ref[...] = seg("sv")
    u_ref[...] = seg("ca") * _sigmoid(seg("cg"))
    lr = _dot(h, wlr_ref[...])
    xa = _dot(lr.astype(BF16), wa2_ref[...]) + ba_ref[...]
    la_ref[...] = _log_sigmoid(xa) * (1.0 / GLA_TAU)


def _mixin(x, g, w_main, w_lr, w_a2, b_a, w_kvt, *, tm, seq_len, kv_t):
    n, d = x.shape
    widths = (GLA_QK, GLA_QK, GLA_V, GLA_V, GLA_QK, SB_W, SB_W, SB_W, CONV_CH)
    const = lambda shape: pl.BlockSpec(shape, lambda i: (0, 0))
    in_specs = [pl.BlockSpec((tm, d), lambda i: (i, 0)), const((1, d)), const(w_main.shape),
                const(w_lr.shape), const(w_a2.shape), const((1, GLA_QK))]
    out_specs = [pl.BlockSpec((tm, w), lambda i: (i, 0)) for w in widths]
    out_shape = [jax.ShapeDtypeStruct((n, w), F32) for w in widths]
    args = [x, g, w_main, w_lr, w_a2, b_a]
    if kv_t:
        assert seq_len % tm == 0
        per_seq = seq_len // tm
        in_specs.append(const(w_kvt.shape))
        args.append(w_kvt)
        for idx in (6, 7):
            out_specs[idx] = pl.BlockSpec((1, SB_W, tm), lambda i: (i // per_seq, 0, i % per_seq))
            out_shape[idx] = jax.ShapeDtypeStruct((n // seq_len, SB_W, seq_len), F32)
    return pl.pallas_call(
        functools.partial(_mixin_body, KV_T=kv_t),
        grid=(n // tm,),
        in_specs=in_specs,
        out_specs=out_specs,
        out_shape=out_shape,
        compiler_params=_cparams(("parallel",)),
        name="mixer_in",
    )(*args)


def _gla_body(q_ref, k_ref, v_ref, la_ref, gg_ref, s0_ref, gn_ref, o_ref, sout_ref, st_scr, b_scr, oi_scr,
              *, C, NB):
    c = pl.program_id(1)

    @pl.when(c == 0)
    def _():
        st_scr[...] = s0_ref[...]

    row = lax.broadcasted_iota(jnp.int32, (C, C), 0)
    col = lax.broadcasted_iota(jnp.int32, (C, C), 1)
    causal = row >= col
    tri = jnp.where(causal, 1.0, 0.0).astype(BF16)
    lane_head = lax.broadcasted_iota(jnp.int32, (C, GLA_QK), 1) // GLA_DK
    heads = [slice(h * GLA_DV, (h + 1) * GLA_DV) for h in range(GLA_HEADS)]

    def head_rows(x, h):
        return jnp.where(lane_head == h, x, 0.0).astype(BF16)

    decays = []
    for n in range(NB):
        q, k, v = q_ref[n], k_ref[n], v_ref[n]
        hi, mid, lo = _split3(la_ref[n])
        b = _dot(tri, jnp.concatenate([hi, mid, lo], axis=1))
        b = b[:, :GLA_QK] + b[:, GLA_QK:2 * GLA_QK] + b[:, 2 * GLA_QK:]
        b_scr[n] = b
        b_last = b[C - 1:C, :]
        decays.append(jnp.max(-b_last))
        st = st_scr[n]
        qb = q * jnp.exp(b)
        kd = k * jnp.exp(b_last - b)
        for h, sl in enumerate(heads):
            oi_scr[n, :, sl] = _dot_nt(head_rows(qb, h), st.astype(BF16))
        upd = jnp.zeros_like(st)
        for h, sl in enumerate(heads):
            upd = upd + _dot_tn(v[:, sl].astype(BF16), head_rows(kd, h))
        st_scr[n] = st * jnp.exp(b_last) + upd
    total_decay = functools.reduce(jnp.maximum, decays)

    @pl.when(total_decay <= GLA_SAFE_DECAY)
    def _():
        for n in range(NB):
            q, k, v, b = q_ref[n], k_ref[n], v_ref[n], b_scr[n]
            qb = q * jnp.exp(b)
            kb = (k * jnp.exp(-b)).astype(BF16)
            for h, sl in enumerate(heads):
                attn = jnp.where(causal, _dot_nt(head_rows(qb, h), kb), 0.0)
                oi_scr[n, :, sl] += _dot(attn.astype(BF16), v[:, sl].astype(BF16))

    @pl.when(total_decay > GLA_SAFE_DECAY)
    def _():
        rhead = lax.broadcasted_iota(jnp.int32, (GLA_QK, GLA_V), 0) // GLA_DK
        chead = lax.broadcasted_iota(jnp.int32, (GLA_QK, GLA_V), 1) // GLA_DV
        spread = jnp.where(rhead == chead, 1.0, 0.0).astype(BF16)
        trow = lax.broadcasted_iota(jnp.int32, (C, GLA_QK), 0)
        for n in range(NB):
            q, b = q_ref[n], b_scr[n]

            def body(s, acc):
                bs = b_scr[n, pl.ds(s, 1), :]
                ks = k_ref[n, pl.ds(s, 1), :]
                vs = v_ref[n, pl.ds(s, 1), :]
                p = q * ks * jnp.exp(jnp.minimum(b - bs, 0.0))
                p = jnp.where(trow >= s, p, 0.0)
                p_hi, p_mid, p_lo = _split3(p)
                r = _dot(p_hi, spread) + _dot(p_mid, spread) + _dot(p_lo, spread)
                return acc + r * vs

            oi_scr[n] += lax.fori_loop(0, C, body, jnp.zeros((C, GLA_V), F32))

    for n in range(NB):
        o, gg = oi_scr[n], gg_ref[n]
        for sl in heads:
            o_ref[n, :, sl] = _rms(o[:, sl], gn_ref[:, sl]) * (gg[:, sl] * _sigmoid(gg[:, sl]))

    @pl.when(c == pl.num_programs(1) - 1)
    def _():
        sout_ref[...] = st_scr[...]


def _gla(q, k, v, la, gg, s0_t, gnorm):
    bsz, t_real, _ = q.shape
    if t_real < 16:
        padt = lambda a: jnp.pad(a, ((0, 0), (0, 16 - t_real), (0, 0)))
        q, k, v, la, gg = (padt(a) for a in (q, k, v, la, gg))
    t = q.shape[1]
    c = min(GLA_CHUNK, t)
    nb = GLA_SEQS_PER_STEP if bsz % GLA_SEQS_PER_STEP == 0 else 1
    assert t % c == 0
    tok = lambda w: pl.BlockSpec((nb, c, w), lambda b, i: (b, i, 0))
    st = pl.BlockSpec((nb, GLA_DV, GLA_QK), lambda b, i: (b, 0, 0))
    o, st_new = pl.pallas_call(
        functools.partial(_gla_body, C=c, NB=nb),
        grid=(bsz // nb, t // c),
        in_specs=[tok(GLA_QK), tok(GLA_QK), tok(GLA_V), tok(GLA_QK), tok(GLA_V), st,
                  pl.BlockSpec((1, GLA_V), lambda b, i: (0, 0))],
        out_specs=[tok(GLA_V), st],
        out_shape=[jax.ShapeDtypeStruct((bsz, t, GLA_V), F32), jax.ShapeDtypeStruct((bsz, GLA_DV, GLA_QK), F32)],
        scratch_shapes=[pltpu.VMEM((nb, GLA_DV, GLA_QK), F32), pltpu.VMEM((nb, c, GLA_QK), F32),
                        pltpu.VMEM((nb, c, GLA_V), F32)],
        compiler_params=_cparams(("parallel", "arbitrary")),
        name="gla",
    )(q, k, v, la, gg, s0_t, gnorm)
    return o[:, :t_real], st_new


def _state_to_t(s):
    b = s.shape[0]
    return jnp.transpose(s, (0, 3, 1, 2)).reshape(b, GLA_DV, GLA_QK)


def _state_from_t(st):
    b = st.shape[0]
    return jnp.transpose(st.reshape(b, GLA_DV, GLA_HEADS, GLA_DK), (0, 2, 3, 1))


def _sb_weights(z, tail_mat_ref, carry, mask):
    ls = _log_sigmoid(z)
    l1m = ls - z
    if mask is not None:
        l1m = jnp.where(mask, l1m, 0.0)
    hi, lo = _split2(l1m)
    cs = _dot(jnp.concatenate([hi, lo], axis=1), tail_mat_ref[...])
    w = jnp.exp(ls + cs[:, :LANES] + carry)
    if mask is not None:
        w = jnp.where(mask, w, 0.0)
    return w, carry + cs[:, LANES:]


def _tail_matrix():
    src = jnp.arange(2 * LANES)[:, None] % LANES
    dst = jnp.arange(2 * LANES)[None, :]
    return jnp.where((dst >= LANES) | (src > dst), 1.0, 0.0).astype(BF16)


def _sb_prompt_body(bias_ref, q_ref, k_ref, v_ref, tm_ref, o_ref, kbf_scr, vbd_scr, qbd_scr, carry_scr, acc_scr,
                    *, T):
    i = pl.program_id(1)
    nkb = T // SB_TILE
    R = SB_HEADS * SB_TILE
    lane_head = lax.broadcasted_iota(jnp.int32, (SB_TILE, SB_W), 1) // SB_DIM

    @pl.when(i == 0)
    def _():
        kbf_scr[...] = k_ref[0].astype(BF16)
        for jb in range(nkb):
            vb = v_ref[0, :, jb * SB_TILE:(jb + 1) * SB_TILE].T
            for h in range(SB_HEADS):
                vbd_scr[jb, h * SB_TILE:(h + 1) * SB_TILE, :] = jnp.where(lane_head == h, vb, 0.0).astype(BF16)

    q = q_ref[0] * (SB_DIM ** -0.5)
    for h in range(SB_HEADS):
        qbd_scr[h * SB_TILE:(h + 1) * SB_TILE, :] = jnp.where(lane_head == h, q, 0.0).astype(BF16)
    carry_scr[...] = jnp.zeros_like(carry_scr)
    acc_scr[...] = jnp.zeros_like(acc_scr)

    def tile(j, mask):
        kblk = kbf_scr[:, pl.ds(pl.multiple_of(j * SB_TILE, SB_TILE), SB_TILE)]
        z = _dot(qbd_scr[...], kblk) + bias_ref[...]
        w, carry_scr[...] = _sb_weights(z, tm_ref, carry_scr[...], mask)
        wb = w.astype(BF16)
        wcat = jnp.concatenate([wb[h * SB_TILE:(h + 1) * SB_TILE] for h in range(SB_HEADS)], axis=1)
        acc_scr[...] += _dot(wcat, vbd_scr[j])

    trow = lax.broadcasted_iota(jnp.int32, (R, SB_TILE), 0) % SB_TILE
    scol = lax.broadcasted_iota(jnp.int32, (R, SB_TILE), 1)
    rem = i % SB_TILES_PER_TRIP
    for r in range(SB_TILES_PER_TRIP):
        @pl.when(rem == r)
        def _(r=r):
            tile(i, scol < trow)
            for u in range(r):
                tile(i - 1 - u, None)

    def group(jj, _):
        for u in range(SB_TILES_PER_TRIP):
            tile(i - rem - 1 - u - SB_TILES_PER_TRIP * jj, None)
        return 0

    lax.fori_loop(0, i // SB_TILES_PER_TRIP, group, 0)

    o_ref[0] = acc_scr[...]


def _sb_prompt(q, k_t, v_t, bias_rows, tail_mat):
    bsz, t, w = q.shape
    nq = t // SB_TILE
    full = pl.BlockSpec((1, w, t), lambda b, i: (b, 0, 0))
    qt = pl.BlockSpec((1, SB_TILE, w), lambda b, i: (b, i, 0))
    const = lambda a: pl.BlockSpec(a.shape, lambda b, i: (0, 0))
    return pl.pallas_call(
        functools.partial(_sb_prompt_body, T=t),
        grid=(bsz, nq),
        in_specs=[const(bias_rows), qt, full, full, const(tail_mat)],
        out_specs=qt,
        out_shape=jax.ShapeDtypeStruct((bsz, t, w), F32),
        scratch_shapes=[pltpu.VMEM((w, t), BF16), pltpu.VMEM((nq, SB_HEADS * SB_TILE, w), BF16),
                        pltpu.VMEM((SB_HEADS * SB_TILE, w), BF16), pltpu.VMEM((SB_HEADS * SB_TILE, LANES), F32),
                        pltpu.VMEM((SB_TILE, w), F32)],
        compiler_params=_cparams(("parallel", "arbitrary")),
        name="sb_prompt",
    )(bias_rows, q, k_t, v_t, tail_mat)


def _sb_decode_body(pt_ref, bias_ref, q_ref, kn_ref, vn_ref, tm_ref, kc_hbm, vc_hbm, o_ref,
                    kbuf, vbuf, sem, knew_scr, vnew_scr, carry_scr, acc_scr, *, TQ, NP, G, NB, NSLOT, LAYER):
    b = pl.program_id(0)
    nb = NB
    ng = NP // G
    R = SB_HEADS * TQ

    def copies(bb, grp, slot):
        out = []
        for p in range(G):
            page = pt_ref[bb, grp * G + p]
            keys = pl.ds(p * SB_TILE, SB_TILE)
            out.append(pltpu.make_async_copy(kc_hbm.at[LAYER, page], kbuf.at[slot, :, keys], sem.at[0, slot]))
            out.append(pltpu.make_async_copy(vc_hbm.at[LAYER, page], vbuf.at[slot, :, keys], sem.at[1, slot]))
        return out

    def start_group(step):
        for cp in copies(step // ng, ng - 1 - step % ng, step % NSLOT):
            cp.start()

    @pl.when(b == 0)
    def _():
        for s in range(min(NSLOT - 1, nb * ng)):
            start_group(s)

    lane_head = lax.broadcasted_iota(jnp.int32, (TQ, SB_W), 1) // SB_DIM
    q = q_ref[0] * (SB_DIM ** -0.5)
    qbd = jnp.concatenate([jnp.where(lane_head == h, q, 0.0) for h in range(SB_HEADS)], axis=0).astype(BF16)
    bias = jnp.concatenate([jnp.full((TQ, LANES), bias_ref[h], F32) for h in range(SB_HEADS)], axis=0)

    knew_scr[...] = jnp.zeros_like(knew_scr)
    vnew_scr[...] = jnp.zeros_like(vnew_scr)
    knew_scr[0:TQ, :] = kn_ref[0]
    vnew_scr[0:TQ, :] = vn_ref[0]
    trow = lax.broadcasted_iota(jnp.int32, (R, LANES), 0) % TQ
    scol = lax.broadcasted_iota(jnp.int32, (R, LANES), 1)
    z = _dot_nt(qbd, knew_scr[...].astype(BF16)) + bias
    w, carry = _sb_weights(z, tm_ref, jnp.zeros((R, LANES), F32), scol < trow)
    carry_scr[...] = carry
    acc_scr[...] = _dot(w.astype(BF16), vnew_scr[...].astype(BF16))

    def group(r, _):
        step = b * ng + r
        slot = step % NSLOT
        for cp in copies(b, ng - 1 - r, slot):
            cp.wait()

        @pl.when(step + NSLOT - 1 < nb * ng)
        def _():
            start_group(step + NSLOT - 1)

        kg = kbuf[slot].astype(BF16)
        vg = vbuf[slot].astype(BF16)
        zg = _dot(qbd, kg)
        zrows = jnp.concatenate([zg[:, p * LANES:(p + 1) * LANES] + bias for p in range(G)], axis=0)
        ls = _log_sigmoid(zrows)
        hi, lo = _split2(ls - zrows)
        cs = _dot(jnp.concatenate([hi, lo], axis=1), tm_ref[...])
        carry = carry_scr[...]
        ws = [None] * G
        for p in reversed(range(G)):
            rs = slice(p * R, (p + 1) * R)
            ws[p] = jnp.exp(ls[rs] + cs[rs, :LANES] + carry).astype(BF16)
            carry = carry + cs[rs, LANES:]
        carry_scr[...] = carry
        acc_scr[...] += _dot_nt(jnp.concatenate(ws, axis=1), vg)
        return 0

    lax.fori_loop(0, ng, group, 0)
    acc = acc_scr[...]
    out = jnp.zeros((TQ, SB_W), F32)
    for h in range(SB_HEADS):
        out = out + jnp.where(lane_head == h, acc[h * TQ:(h + 1) * TQ, :], 0.0)
    o_ref[0] = out


def _sb_decode(q, k_new, v_new, bias, tail_mat, cache_k, cache_v, page_table, layer):
    db, tq, w = q.shape
    npages = page_table.shape[1]
    g = math.gcd(npages, PAGES_PER_GROUP)
    assert cache_k.shape[2:] == (w, SB_TILE)
    tok = pl.BlockSpec((1, tq, w), lambda b, pt: (b, 0, 0))
    grid_spec = pltpu.PrefetchScalarGridSpec(
        num_scalar_prefetch=1,
        grid=(db,),
        in_specs=[pl.BlockSpec(memory_space=pltpu.SMEM), tok, tok, tok,
                  pl.BlockSpec(tail_mat.shape, lambda b, pt: (0, 0)),
                  pl.BlockSpec(memory_space=pl.ANY), pl.BlockSpec(memory_space=pl.ANY)],
        out_specs=tok,
        scratch_shapes=[pltpu.VMEM((DECODE_SLOTS, w, g * SB_TILE), F32), pltpu.VMEM((DECODE_SLOTS, w, g * SB_TILE), F32),
                        pltpu.SemaphoreType.DMA((2, DECODE_SLOTS)),
                        pltpu.VMEM((SB_TILE, w), F32), pltpu.VMEM((SB_TILE, w), F32),
                        pltpu.VMEM((SB_HEADS * tq, LANES), F32), pltpu.VMEM((SB_HEADS * tq, w), F32)],
    )
    return pl.pallas_call(
        functools.partial(_sb_decode_body, TQ=tq, NP=npages, G=g, NB=db, NSLOT=DECODE_SLOTS,
                          LAYER=layer),
        grid_spec=grid_spec,
        out_shape=jax.ShapeDtypeStruct((db, tq, w), F32),
        compiler_params=_cparams(("arbitrary",)),
        name="sb_decode",
    )(page_table, bias, q, k_new, v_new, tail_mat, cache_k, cache_v)


def _conv_body(u_ref, prev_ref, buf_ref, w_ref, cb_ref, lg_ref, lb_ref, y_ref, bufo_ref, win_scr, *, RT, NT):
    i = pl.program_id(1)
    hist = CONV_WIDTH - 1
    first = HALO - hist

    @pl.when(i == 0)
    def _():
        win_scr[first:HALO, :] = buf_ref[0]

    if NT > 1:
        @pl.when(i > 0)
        def _():
            win_scr[0:HALO, :] = prev_ref[0]

    win_scr[HALO:HALO + RT, :] = u_ref[0]
    st = min(RT, CONV_SUBTILE)
    n = HALO + st
    for base in range(0, RT, st):
        win = win_scr[base:base + n, :]
        acc = jnp.zeros((st, CONV_CH), F32)
        for r in range(SUBLANES):
            offs = [o for o in range(first, first + CONV_WIDTH) if o % SUBLANES == r]
            shifted = win if r == 0 else pltpu.roll(win, n - r, 0)
            for o in offs:
                acc = acc + shifted[o - r:o - r + st, :] * w_ref[o - first:o - first + 1, :]
        y = acc + cb_ref[...]
        mu = jnp.mean(y, axis=-1, keepdims=True)
        yc = y - mu
        yn = yc * lax.rsqrt(jnp.mean(yc * yc, axis=-1, keepdims=True) + EPS) * lg_ref[...] + lb_ref[...]
        y_ref[0, base:base + st, :] = yn * _sigmoid(yn)

    @pl.when(i == pl.num_programs(1) - 1)
    def _():
        bufo_ref[0] = win_scr[first + RT:HALO + RT, :]


def _conv(u, buf, w, cb, lg, lb):
    bsz, t, ch = u.shape
    hist = CONV_WIDTH - 1
    rt = min(t, CONV_TILE)
    nt = t // rt
    assert t % rt == 0 and (nt == 1 or rt % HALO == 0)
    per = rt // HALO if nt > 1 else 1
    tile = pl.BlockSpec((1, rt, ch), lambda b, i: (b, i, 0))
    prev_rows = HALO if nt > 1 else rt
    prev = pl.BlockSpec((1, prev_rows, ch), lambda b, i: (b, jnp.maximum(i * per - 1, 0), 0))
    state = pl.BlockSpec((1, hist, ch), lambda b, i: (b, 0, 0))
    vec = lambda r: pl.BlockSpec((r, ch), lambda b, i: (0, 0))
    return pl.pallas_call(
        functools.partial(_conv_body, RT=rt, NT=nt),
        grid=(bsz, nt),
        in_specs=[tile, prev, state, vec(CONV_WIDTH), vec(1), vec(1), vec(1)],
        out_specs=[tile, state],
        out_shape=[jax.ShapeDtypeStruct((bsz, t, ch), F32), jax.ShapeDtypeStruct((bsz, hist, ch), F32)],
        scratch_shapes=[pltpu.VMEM((HALO + rt, ch), F32)],
        compiler_params=_cparams(("parallel", "arbitrary")),
        name="conv_branch",
    )(u, u, buf, w, cb, lg, lb)


def _mixout_body(x_ref, og_ref, osb_ref, y_ref, w_ref, g_ref, o_ref):
    m = _dot(og_ref[...].astype(BF16), w_ref[0:GLA_V, :])
    m = m + _dot(osb_ref[...].astype(BF16), w_ref[GLA_V:GLA_V + SB_W, :])
    m = m + _dot(y_ref[...].astype(BF16), w_ref[GLA_V + SB_W:, :])
    o_ref[...] = x_ref[...] + _rms(m, g_ref[...])


def _mixout(x, og, osb, y, w_out, g, layer, *, tm):
    n, d = x.shape
    tok = lambda w: pl.BlockSpec((tm, w), lambda i: (i, 0))
    return pl.pallas_call(
        _mixout_body,
        grid=(n // tm,),
        in_specs=[tok(d), tok(GLA_V), tok(SB_W), tok(CONV_CH),
                  pl.BlockSpec((None,) + w_out.shape[1:], lambda i: (layer, 0, 0)),
                  pl.BlockSpec((None, 1, d), lambda i: (layer, 0, 0))],
        out_specs=tok(d),
        out_shape=jax.ShapeDtypeStruct((n, d), F32),
        compiler_params=_cparams(("parallel",)),
        name="mixer_out",
    )(x, og, osb, y, w_out, g)


def _layer_weights(l, w):
    d = w["w_in"].shape[1]
    w_in = w["w_in"][l]
    sizes = (GLA_QK, GLA_QK, GLA_V, GLA_V, GLA_RANK, SB_W, SB_W, SB_W, 2 * CONV_CH)
    offs = [0]
    for s in sizes:
        offs.append(offs[-1] + s)
    main = jnp.concatenate([w_in[:, :offs[4]], w_in[:, offs[5]:]], axis=1).astype(BF16)
    w_lr = jnp.zeros((d, LANES), F32).at[:, :GLA_RANK].set(w_in[:, offs[4]:offs[5]]).astype(BF16)
    w_a2 = jnp.zeros((LANES, GLA_QK), F32).at[:GLA_RANK, :].set(w["gla_w_a2"][l]).astype(BF16)
    row = lambda name: w[name][l][None, :]
    return dict(
        layer=l,
        ffn1=(w["ffn1_norm_pre"][:, None, :], w["ffn1_norm_post"][:, None, :], w["ffn1_w_gate_bf"], w["ffn1_w_up_bf"], w["ffn1_w_down_bf"]),
        ffn2=(w["ffn2_norm_pre"][:, None, :], w["ffn2_norm_post"][:, None, :], w["ffn2_w_gate_bf"], w["ffn2_w_up_bf"], w["ffn2_w_down_bf"]),
        mixin=(row("mix_norm_pre"), main, w_lr, w_a2, row("gla_b_a"), w_in[:, offs[6]:offs[8]].T.astype(BF16)),
        gla_norm=row("gla_norm"),
        sb_bias=w["sb_bias"][l],
        sb_bias_rows=jnp.broadcast_to(jnp.repeat(w["sb_bias"][l], SB_TILE)[:, None], (SB_HEADS * SB_TILE, LANES)),
        conv=(w["conv_w"][l], row("conv_b"), row("conv_norm_g"), row("conv_norm_b")),
        w_out=w["w_out_bf"],
        mix_post=w["mix_norm_post"][:, None, :],
    )


def _token_tile(n):
    for tm in (1024, 512, 256, 128, 64, 32, 16, 8):
        if n % tm == 0:
            return tm
    raise ValueError(n)


def _ffn_tile(f):
    for tf in (256, 128):
        if f % tf == 0:
            return tf
    return f


def _group_layer(x, lw, s0, buf, tail_mat, attend, kv_t):
    bsz, t, d = x.shape
    n = bsz * t
    tm = _token_tile(n)
    tf = _ffn_tile(lw["ffn1"][2].shape[2])
    x2 = _ffn(x.reshape(n, d), *lw["ffn1"], lw["layer"], tm=tm, tf=tf)
    gq, gk, gv, gg, la, sq, sk, sv, u = _mixin(x2, *lw["mixin"], tm=min(tm, 512, t) if kv_t else min(tm, 512),
                                               seq_len=t, kv_t=kv_t)
    seq = lambda a: a.reshape(bsz, t, a.shape[-1])
    o_gla, st = _gla(seq(gq), seq(gk), seq(gv), seq(la), seq(gg), _state_to_t(s0), lw["gla_norm"])
    o_sb = attend(seq(sq), sk, sv) if kv_t else attend(seq(sq), seq(sk), seq(sv))
    y, buf_new = _conv(seq(u), buf, *lw["conv"])
    x3 = _mixout(x2, o_gla.reshape(n, GLA_V), o_sb.reshape(n, SB_W), y.reshape(n, CONV_CH), lw["w_out"],
                 lw["mix_post"], lw["layer"], tm=min(tm, 512))
    x4 = _ffn(x3, *lw["ffn2"], lw["layer"], tm=tm, tf=tf)
    if kv_t:
        heads = lambda a: a.reshape(bsz, SB_HEADS, SB_DIM, t)
    else:
        heads = lambda a: a.reshape(bsz, t, SB_HEADS, SB_DIM)
    return x4.reshape(bsz, t, d), heads(sk), heads(sv), _state_from_t(st), buf_new


def kernel(x_prompt, x_sample, cache_sb_k, cache_sb_v, page_table, state_gla, state_conv, ffn1_norm_pre, ffn1_norm_post, ffn1_w_gate, ffn1_w_up, ffn1_w_down, mix_norm_pre, mix_norm_post, w_in, gla_w_a2, gla_b_a, gla_norm, sb_bias, conv_w, conv_b, conv_norm_g, conv_norm_b, w_out, ffn2_norm_pre, ffn2_norm_post, ffn2_w_gate, ffn2_w_up, ffn2_w_down):
    weights = dict(ffn1_norm_pre=ffn1_norm_pre, ffn1_norm_post=ffn1_norm_post, ffn1_w_gate=ffn1_w_gate,
                   ffn1_w_up=ffn1_w_up, ffn1_w_down=ffn1_w_down, mix_norm_pre=mix_norm_pre,
                   mix_norm_post=mix_norm_post, w_in=w_in, gla_w_a2=gla_w_a2, gla_b_a=gla_b_a, gla_norm=gla_norm,
                   sb_bias=sb_bias, conv_w=conv_w, conv_b=conv_b, conv_norm_g=conv_norm_g,
                   conv_norm_b=conv_norm_b, w_out=w_out, ffn2_norm_pre=ffn2_norm_pre,
                   ffn2_norm_post=ffn2_norm_post, ffn2_w_gate=ffn2_w_gate, ffn2_w_up=ffn2_w_up,
                   ffn2_w_down=ffn2_w_down)
    for name in ("ffn1_w_gate", "ffn1_w_up", "ffn1_w_down", "ffn2_w_gate", "ffn2_w_up", "ffn2_w_down", "w_out"):
        weights[name + "_bf"] = weights[name].astype(BF16)
    depth = w_in.shape[0]
    bsz = x_prompt.shape[0]
    n_pool, page = cache_sb_k.shape[1], cache_sb_k.shape[2]
    tail_mat = _tail_matrix()
    ck = jnp.transpose(cache_sb_k, (0, 1, 3, 4, 2)).reshape(depth, n_pool, SB_W, page)
    cv = jnp.transpose(cache_sb_v, (0, 1, 3, 4, 2)).reshape(depth, n_pool, SB_W, page)
    xp, xs = x_prompt, x_sample
    outs = [[] for _ in range(8)]
    for l in range(depth):
        lw = _layer_weights(l, weights)
        s0 = jnp.zeros((bsz, GLA_HEADS, GLA_DK, GLA_DV), F32)
        buf0 = jnp.zeros((bsz, CONV_WIDTH - 1, CONV_CH), F32)
        prompt_attend = lambda q, k, v: _sb_prompt(q, k, v, lw["sb_bias_rows"], tail_mat)
        xp, k1, v1, s1, c1 = _group_layer(xp, lw, s0, buf0, tail_mat, prompt_attend, True)
        decode_attend = lambda q, k, v, l=l: _sb_decode(q, k, v, lw["sb_bias"], tail_mat, ck, cv, page_table, l)
        xs, k2, v2, s2, c2 = _group_layer(xs, lw, state_gla[l], state_conv[l], tail_mat, decode_attend, False)
        for lst, val in zip(outs, (k1, v1, s1, c1, k2, v2, s2, c2)):
            lst.append(val)
    stacked = [jnp.stack(o) for o in outs]
    for idx in (0, 1):
        stacked[idx] = jnp.transpose(stacked[idx], (0, 1, 4, 2, 3))
    return (xp, xs) + tuple(stacked)
```

```python
import functools
import math

import jax
import jax.numpy as jnp
from jax import lax
from jax.experimental import pallas as pl
from jax.experimental.pallas import tpu as pltpu

F32 = jnp.float32
BF16 = jnp.bfloat16

EPS = 1e-6
GLA_HEADS = 4
GLA_DK = 64
GLA_DV = 128
GLA_RANK = 16
GLA_TAU = 16.0
GLA_CHUNK = 128
SB_HEADS = 4
SB_DIM = 64
CONV_CH = 256
CONV_WIDTH = 31
GLA_QK = GLA_HEADS * GLA_DK
GLA_V = GLA_HEADS * GLA_DV
SB_W = SB_HEADS * SB_DIM

LANES = 128
SUBLANES = 8
SB_TILE = 128
SB_TILES_PER_TRIP = 4
HALO = 32
CONV_TILE = 256
CONV_SUBTILE = 64
GLA_SAFE_DECAY = 60.0
GLA_SEQS_PER_STEP = 4
PAGES_PER_GROUP = 16
DECODE_SLOTS = 4
VMEM_LIMIT = 48 * 1024 * 1024
FFN_VMEM_LIMIT = 56 * 1024 * 1024


def _cparams(sem):
    return pltpu.CompilerParams(dimension_semantics=sem, vmem_limit_bytes=VMEM_LIMIT)


def _rms(x, g):
    return x * lax.rsqrt(jnp.mean(x * x, axis=-1, keepdims=True) + EPS) * g


def _sigmoid(x):
    return 1.0 / (1.0 + jnp.exp(-x))


def _log_sigmoid(x):
    return jnp.minimum(x, 0.0) - jnp.log(1.0 + jnp.exp(-jnp.abs(x)))


def _split2(x):
    hi = x.astype(BF16)
    lo = (x - hi.astype(F32)).astype(BF16)
    return hi, lo


def _split3(x):
    hi = x.astype(BF16)
    r = x - hi.astype(F32)
    mid = r.astype(BF16)
    lo = (r - mid.astype(F32)).astype(BF16)
    return hi, mid, lo


def _dot(a, b):
    return jnp.dot(a, b, preferred_element_type=F32)


def _dot_nt(a, b):
    return lax.dot_general(a, b, (((1,), (1,)), ((), ())), preferred_element_type=F32)


def _dot_tn(a, b):
    return lax.dot_general(a, b, (((0,), (0,)), ((), ())), preferred_element_type=F32)


def _ffn_body(*refs, TF, MIX):
    if MIX:
        x_in_ref, og_ref, osb_ref, y_ref, wout_ref, gmix_ref = refs[:6]
        gpre_ref, gpost_ref, wg_ref, wu_ref, wd_ref, o_ref, h_scr, acc_scr = refs[6:]
        m = _dot(og_ref[...].astype(BF16), wout_ref[0:GLA_V, :])
        m = m + _dot(osb_ref[...].astype(BF16), wout_ref[GLA_V:GLA_V + SB_W, :])
        m = m + _dot(y_ref[...].astype(BF16), wout_ref[GLA_V + SB_W:, :])
        o_ref[...] = x_in_ref[...] + _rms(m, gmix_ref[...])
        x_ref = o_ref
    else:
        x_ref, gpre_ref, gpost_ref, wg_ref, wu_ref, wd_ref, o_ref, h_scr, acc_scr = refs
    h_scr[...] = _rms(x_ref[...], gpre_ref[...]).astype(BF16)
    nf = wg_ref.shape[1] // TF

    def chunk(f):
        cols = pl.ds(pl.multiple_of(f * TF, TF), TF)
        h = h_scr[...]
        g = _dot(h, wg_ref[:, cols])
        u = _dot(h, wu_ref[:, cols])
        a = (g * _sigmoid(g)) * u
        return _dot(a.astype(BF16), wd_ref[cols, :])

    acc_scr[...] = chunk(0)

    def body(f, carry):
        acc_scr[...] += chunk(f)
        return carry

    lax.fori_loop(1, nf, body, 0, unroll=2 if (nf - 1) % 2 == 0 else 1)
    o_ref[...] = x_ref[...] + 0.5 * _rms(acc_scr[...], gpost_ref[...])


def _ffn(x, g_pre, g_post, w_gate, w_up, w_down, layer, *, tm, tf, mix=None):
    n, d = x.shape
    resident = lambda a: pl.BlockSpec((None,) + a.shape[1:], lambda i: (layer, 0, 0), pipeline_mode=pl.Buffered(1))
    gain = pl.BlockSpec((None, 1, d), lambda i: (layer, 0, 0))
    tok = lambda w: pl.BlockSpec((tm, w), lambda i: (i, 0))
    in_specs, args = [tok(d)], [x]
    if mix is not None:
        og, osb, y, w_out, g_mix = mix
        in_specs += [tok(GLA_V), tok(SB_W), tok(CONV_CH), resident(w_out), gain]
        args += [og, osb, y, w_out, g_mix]
    in_specs += [gain, gain, resident(w_gate), resident(w_up), resident(w_down)]
    args += [g_pre, g_post, w_gate, w_up, w_down]
    return pl.pallas_call(
        functools.partial(_ffn_body, TF=tf, MIX=mix is not None),
        grid=(n // tm,),
        in_specs=in_specs,
        out_specs=tok(d),
        out_shape=jax.ShapeDtypeStruct((n, d), F32),
        scratch_shapes=[pltpu.VMEM((tm, d), BF16), pltpu.VMEM((tm, d), F32)],
        compiler_params=pltpu.CompilerParams(dimension_semantics=("parallel",),
                                             vmem_limit_bytes=FFN_VMEM_LIMIT if mix is not None else VMEM_LIMIT),
        name="half_ffn" if mix is None else "mixer_out_ffn",
    )(*args)


_SEG = {}
_off = 0
for _name, _w in (("gq", GLA_QK), ("gk", GLA_QK), ("gv", GLA_V), ("gg", GLA_V), ("sq", SB_W), ("sk", SB_W),
                  ("sv", SB_W), ("ca", CONV_CH), ("cg", CONV_CH)):
    _SEG[_name] = (_off, _off + _w)
    _off += _w
MAIN_COLS = _off


def _mixin_body(x_ref, g_ref, w_ref, wlr_ref, wa2_ref, ba_ref, *rest, KV_T):
    if KV_T:
        wkvt_ref, rest = rest[0], rest[1:]
    gq_ref, gk_ref, gv_ref, gg_ref, la_ref, sq_ref, sk_ref, sv_ref, u_ref = rest
    h = _rms(x_ref[...], g_ref[...]).astype(BF16)

    def seg(name):
        lo, hi = _SEG[name]
        return _dot(h, w_ref[:, lo:hi])

    gq_ref[...] = seg("gq") * (GLA_DK ** -0.5)
    gk_ref[...] = seg("gk")
    gv_ref[...] = seg("gv")
    gg_ref[...] = seg("gg")
    sq_ref[...] = seg("sq")
    if KV_T:
        sk_ref[0] = _dot_nt(wkvt_ref[0:SB_W, :], h)
        sv_ref[0] = _dot_nt(wkvt_ref[SB_W:2 * SB_W, :], h)
    else:
        sk_ref[...] = seg("sk")
        sv_ref[...] = seg("sv")
    u_ref[...] = seg("ca") * _sigmoid(seg("cg"))
    lr = _dot(h, wlr_ref[...])
    xa = _dot(lr.astype(BF16), wa2_ref[...]) + ba_ref[...]
    la_ref[...] = _log_sigmoid(xa) * (1.0 / GLA_TAU)


def _mixin(x, g, w_main, w_lr, w_a2, b_a, w_kvt, *, tm, seq_len, kv_t):
    n, d = x.shape
    widths = (GLA_QK, GLA_QK, GLA_V, GLA_V, GLA_QK, SB_W, SB_W, SB_W, CONV_CH)
    const = lambda shape: pl.BlockSpec(shape, lambda i: (0, 0))
    in_specs = [pl.BlockSpec((tm, d), lambda i: (i, 0)), const((1, d)), const(w_main.shape),
                const(w_lr.shape), const(w_a2.shape), const((1, GLA_QK))]
    out_specs = [pl.BlockSpec((tm, w), lambda i: (i, 0)) for w in widths]
    out_shape = [jax.ShapeDtypeStruct((n, w), F32) for w in widths]
    args = [x, g, w_main, w_lr, w_a2, b_a]
    if kv_t:
        assert seq_len % tm == 0
        per_seq = seq_len // tm
        in_specs.append(const(w_kvt.shape))
        args.append(w_kvt)
        for idx in (6, 7):
            out_specs[idx] = pl.BlockSpec((1, SB_W, tm), lambda i: (i // per_seq, 0, i % per_seq))
            out_shape[idx] = jax.ShapeDtypeStruct((n // seq_len, SB_W, seq_len), F32)
    return pl.pallas_call(
        functools.partial(_mixin_body, KV_T=kv_t),
        grid=(n // tm,),
        in_specs=in_specs,
        out_specs=out_specs,
        out_shape=out_shape,
        compiler_params=_cparams(("parallel",)),
        name="mixer_in",
    )(*args)


def _gla_body(q_ref, k_ref, v_ref, la_ref, gg_ref, s0_ref, gn_ref, o_ref, sout_ref, st_scr, b_scr, oi_scr, qbs_scr,
              *, C, NB):
    c = pl.program_id(1)

    @pl.when(c == 0)
    def _():
        st_scr[...] = s0_ref[...]

    row = lax.broadcasted_iota(jnp.int32, (C, C), 0)
    col = lax.broadcasted_iota(jnp.int32, (C, C), 1)
    causal = row >= col
    tri = jnp.where(causal, 1.0, 0.0).astype(BF16)
    lane_head = lax.broadcasted_iota(jnp.int32, (C, GLA_QK), 1) // GLA_DK
    heads = [slice(h * GLA_DV, (h + 1) * GLA_DV) for h in range(GLA_HEADS)]

    def head_stack(x):
        return jnp.concatenate([jnp.where(lane_head == h, x, 0.0) for h in range(GLA_HEADS)], axis=0).astype(BF16)

    rows_t = lax.broadcasted_iota(jnp.int32, (GLA_HEADS * C, C), 0) % C
    causal_rows = rows_t >= lax.broadcasted_iota(jnp.int32, (GLA_HEADS * C, C), 1)

    decays = []
    for n in range(NB):
        q, k, v = q_ref[n], k_ref[n], v_ref[n]
        hi, mid, lo = _split3(la_ref[n])
        b = _dot(tri, jnp.concatenate([hi, mid, lo], axis=1))
        b = b[:, :GLA_QK] + b[:, GLA_QK:2 * GLA_QK] + b[:, 2 * GLA_QK:]
        b_scr[n] = b
        b_last = b[C - 1:C, :]
        decays.append(jnp.max(-b_last))
        st = st_scr[n]
        qb = q * jnp.exp(b)
        kd = k * jnp.exp(b_last - b)
        qbs = head_stack(qb)
        qbs_scr[n] = qbs
        oi = _dot_nt(qbs, st.astype(BF16))
        for h, sl in enumerate(heads):
            oi_scr[n, :, sl] = oi[h * C:(h + 1) * C]
        v_rows = jnp.concatenate([v[:, sl] for sl in heads], axis=0).astype(BF16)
        st_scr[n] = st * jnp.exp(b_last) + _dot_tn(v_rows, head_stack(kd))
    total_decay = functools.reduce(jnp.maximum, decays)

    @pl.when(total_decay <= GLA_SAFE_DECAY)
    def _():
        for n in range(NB):
            k, v, b = k_ref[n], v_ref[n], b_scr[n]
            kb = (k * jnp.exp(-b)).astype(BF16)
            attn = jnp.where(causal_rows, _dot_nt(qbs_scr[n], kb), 0.0).astype(BF16)
            for h, sl in enumerate(heads):
                oi_scr[n, :, sl] += _dot(attn[h * C:(h + 1) * C], v[:, sl].astype(BF16))

    @pl.when(total_decay > GLA_SAFE_DECAY)
    def _():
        rhead = lax.broadcasted_iota(jnp.int32, (GLA_QK, GLA_V), 0) // GLA_DK
        chead = lax.broadcasted_iota(jnp.int32, (GLA_QK, GLA_V), 1) // GLA_DV
        spread = jnp.where(rhead == chead, 1.0, 0.0).astype(BF16)
        trow = lax.broadcasted_iota(jnp.int32, (C, GLA_QK), 0)
        for n in range(NB):
            q, b = q_ref[n], b_scr[n]

            def body(s, acc):
                bs = b_scr[n, pl.ds(s, 1), :]
                ks = k_ref[n, pl.ds(s, 1), :]
                vs = v_ref[n, pl.ds(s, 1), :]
                p = q * ks * jnp.exp(jnp.minimum(b - bs, 0.0))
                p = jnp.where(trow >= s, p, 0.0)
                p_hi, p_mid, p_lo = _split3(p)
                r = _dot(p_hi, spread) + _dot(p_mid, spread) + _dot(p_lo, spread)
                return acc + r * vs

            oi_scr[n] += lax.fori_loop(0, C, body, jnp.zeros((C, GLA_V), F32))

    for n in range(NB):
        o, gg = oi_scr[n], gg_ref[n]
        for sl in heads:
            o_ref[n, :, sl] = _rms(o[:, sl], gn_ref[:, sl]) * (gg[:, sl] * _sigmoid(gg[:, sl]))

    @pl.when(c == pl.num_programs(1) - 1)
    def _():
        sout_ref[...] = st_scr[...]


def _gla(q, k, v, la, gg, s0_t, gnorm):
    bsz, t_real, _ = q.shape
    if t_real < 16:
        padt = lambda a: jnp.pad(a, ((0, 0), (0, 16 - t_real), (0, 0)))
        q, k, v, la, gg = (padt(a) for a in (q, k, v, la, gg))
    t = q.shape[1]
    c = min(GLA_CHUNK, t)
    nb = GLA_SEQS_PER_STEP if bsz % GLA_SEQS_PER_STEP == 0 else 1
    assert t % c == 0
    tok = lambda w: pl.BlockSpec((nb, c, w), lambda b, i: (b, i, 0))
    st = pl.BlockSpec((nb, GLA_DV, GLA_QK), lambda b, i: (b, 0, 0))
    o, st_new = pl.pallas_call(
        functools.partial(_gla_body, C=c, NB=nb),
        grid=(bsz // nb, t // c),
        in_specs=[tok(GLA_QK), tok(GLA_QK), tok(GLA_V), tok(GLA_QK), tok(GLA_V), st,
                  pl.BlockSpec((1, GLA_V), lambda b, i: (0, 0))],
        out_specs=[tok(GLA_V), st],
        out_shape=[jax.ShapeDtypeStruct((bsz, t, GLA_V), F32), jax.ShapeDtypeStruct((bsz, GLA_DV, GLA_QK), F32)],
        scratch_shapes=[pltpu.VMEM((nb, GLA_DV, GLA_QK), F32), pltpu.VMEM((nb, c, GLA_QK), F32),
                        pltpu.VMEM((nb, c, GLA_V), F32), pltpu.VMEM((nb, GLA_HEADS * c, GLA_QK), BF16)],
        compiler_params=_cparams(("parallel", "arbitrary")),
        name="gla",
    )(q, k, v, la, gg, s0_t, gnorm)
    return o[:, :t_real], st_new


def _state_to_t(s):
    b = s.shape[0]
    return jnp.transpose(s, (0, 3, 1, 2)).reshape(b, GLA_DV, GLA_QK)


def _state_from_t(st):
    b = st.shape[0]
    return jnp.transpose(st.reshape(b, GLA_DV, GLA_HEADS, GLA_DK), (0, 2, 3, 1))


def _sb_weights(z, tail_mat_ref, carry, mask):
    ls = _log_sigmoid(z)
    l1m = ls - z
    if mask is not None:
        l1m = jnp.where(mask, l1m, 0.0)
    hi, lo = _split2(l1m)
    cs = _dot(jnp.concatenate([hi, lo], axis=1), tail_mat_ref[...])
    w = jnp.exp(ls + cs[:, :LANES] + carry)
    if mask is not None:
        w = jnp.where(mask, w, 0.0)
    return w, carry + cs[:, LANES:]


def _tail_matrix():
    src = jnp.arange(2 * LANES)[:, None] % LANES
    dst = jnp.arange(2 * LANES)[None, :]
    return jnp.where((dst >= LANES) | (src > dst), 1.0, 0.0).astype(BF16)


def _sb_prompt_body(bias_ref, q_ref, k_ref, v_ref, tm_ref, o_ref, kbf_scr, vbd_scr, qbd_scr, carry_scr, acc_scr,
                    *, T):
    i = pl.program_id(1)
    nkb = T // SB_TILE
    R = SB_HEADS * SB_TILE
    lane_head = lax.broadcasted_iota(jnp.int32, (SB_TILE, SB_W), 1) // SB_DIM

    @pl.when(i == 0)
    def _():
        kbf_scr[...] = k_ref[0].astype(BF16)
        for jb in range(nkb):
            vb = v_ref[0, :, jb * SB_TILE:(jb + 1) * SB_TILE].T
            for h in range(SB_HEADS):
                vbd_scr[jb, h * SB_TILE:(h + 1) * SB_TILE, :] = jnp.where(lane_head == h, vb, 0.0).astype(BF16)

    q = q_ref[0] * (SB_DIM ** -0.5)
    for h in range(SB_HEADS):
        qbd_scr[h * SB_TILE:(h + 1) * SB_TILE, :] = jnp.where(lane_head == h, q, 0.0).astype(BF16)
    carry_scr[...] = jnp.zeros_like(carry_scr)
    acc_scr[...] = jnp.zeros_like(acc_scr)

    def tile(j, mask):
        kblk = kbf_scr[:, pl.ds(pl.multiple_of(j * SB_TILE, SB_TILE), SB_TILE)]
        z = _dot(qbd_scr[...], kblk) + bias_ref[...]
        w, carry_scr[...] = _sb_weights(z, tm_ref, carry_scr[...], mask)
        wb = w.astype(BF16)
        wcat = jnp.concatenate([wb[h * SB_TILE:(h + 1) * SB_TILE] for h in range(SB_HEADS)], axis=1)
        acc_scr[...] += _dot(wcat, vbd_scr[j])

    trow = lax.broadcasted_iota(jnp.int32, (R, SB_TILE), 0) % SB_TILE
    scol = lax.broadcasted_iota(jnp.int32, (R, SB_TILE), 1)
    rem = i % SB_TILES_PER_TRIP
    for r in range(SB_TILES_PER_TRIP):
        @pl.when(rem == r)
        def _(r=r):
            tile(i, scol < trow)
            for u in range(r):
                tile(i - 1 - u, None)

    def group(jj, _):
        for u in range(SB_TILES_PER_TRIP):
            tile(i - rem - 1 - u - SB_TILES_PER_TRIP * jj, None)
        return 0

    lax.fori_loop(0, i // SB_TILES_PER_TRIP, group, 0)

    o_ref[0] = acc_scr[...]


def _sb_prompt(q, k_t, v_t, bias_rows, tail_mat):
    bsz, t, w = q.shape
    nq = t // SB_TILE
    full = pl.BlockSpec((1, w, t), lambda b, i: (b, 0, 0))
    qt = pl.BlockSpec((1, SB_TILE, w), lambda b, i: (b, i, 0))
    const = lambda a: pl.BlockSpec(a.shape, lambda b, i: (0, 0))
    return pl.pallas_call(
        functools.partial(_sb_prompt_body, T=t),
        grid=(bsz, nq),
        in_specs=[const(bias_rows), qt, full, full, const(tail_mat)],
        out_specs=qt,
        out_shape=jax.ShapeDtypeStruct((bsz, t, w), F32),
        scratch_shapes=[pltpu.VMEM((w, t), BF16), pltpu.VMEM((nq, SB_HEADS * SB_TILE, w), BF16),
                        pltpu.VMEM((SB_HEADS * SB_TILE, w), BF16), pltpu.VMEM((SB_HEADS * SB_TILE, LANES), F32),
                        pltpu.VMEM((SB_TILE, w), F32)],
        compiler_params=_cparams(("parallel", "arbitrary")),
        name="sb_prompt",
    )(bias_rows, q, k_t, v_t, tail_mat)


def _sb_decode_body(pt_ref, bias_ref, q_ref, kn_ref, vn_ref, tm_ref, kc_hbm, vc_hbm, o_ref,
                    kbuf, vbuf, sem, knew_scr, vnew_scr, carry_scr, acc_scr, *, TQ, NP, G, NB, NSLOT, LAYER):
    b = pl.program_id(0)
    nb = NB
    ng = NP // G
    R = SB_HEADS * TQ

    def copies(bb, grp, slot):
        out = []
        for p in range(G):
            page = pt_ref[bb, grp * G + p]
            keys = pl.ds(p * SB_TILE, SB_TILE)
            out.append(pltpu.make_async_copy(kc_hbm.at[LAYER, page], kbuf.at[slot, :, keys], sem.at[0, slot]))
            out.append(pltpu.make_async_copy(vc_hbm.at[LAYER, page], vbuf.at[slot, :, keys], sem.at[1, slot]))
        return out

    def start_group(step):
        for cp in copies(step // ng, ng - 1 - step % ng, step % NSLOT):
            cp.start()

    @pl.when(b == 0)
    def _():
        for s in range(min(NSLOT - 1, nb * ng)):
            start_group(s)

    lane_head = lax.broadcasted_iota(jnp.int32, (TQ, SB_W), 1) // SB_DIM
    q = q_ref[0] * (SB_DIM ** -0.5)
    qbd = jnp.concatenate([jnp.where(lane_head == h, q, 0.0) for h in range(SB_HEADS)], axis=0).astype(BF16)
    bias = jnp.concatenate([jnp.full((TQ, LANES), bias_ref[h], F32) for h in range(SB_HEADS)], axis=0)

    knew_scr[...] = jnp.zeros_like(knew_scr)
    vnew_scr[...] = jnp.zeros_like(vnew_scr)
    knew_scr[0:TQ, :] = kn_ref[0]
    vnew_scr[0:TQ, :] = vn_ref[0]
    trow = lax.broadcasted_iota(jnp.int32, (R, LANES), 0) % TQ
    scol = lax.broadcasted_iota(jnp.int32, (R, LANES), 1)
    z = _dot_nt(qbd, knew_scr[...].astype(BF16)) + bias
    w, carry = _sb_weights(z, tm_ref, jnp.zeros((R, LANES), F32), scol < trow)
    carry_scr[...] = carry
    acc_scr[...] = _dot(w.astype(BF16), vnew_scr[...].astype(BF16))

    def group(r, _):
        step = b * ng + r
        slot = step % NSLOT
        for cp in copies(b, ng - 1 - r, slot):
            cp.wait()

        @pl.when(step + NSLOT - 1 < nb * ng)
        def _():
            start_group(step + NSLOT - 1)

        kg = kbuf[slot].astype(BF16)
        vg = vbuf[slot].astype(BF16)
        zg = _dot(qbd, kg)
        zrows = jnp.concatenate([zg[:, p * LANES:(p + 1) * LANES] + bias for p in range(G)], axis=0)
        ls = _log_sigmoid(zrows)
        hi, lo = _split2(ls - zrows)
        cs = _dot(jnp.concatenate([hi, lo], axis=1), tm_ref[...])
        carry = carry_scr[...]
        ws = [None] * G
        for p in reversed(range(G)):
            rs = slice(p * R, (p + 1) * R)
            ws[p] = jnp.exp(ls[rs] + cs[rs, :LANES] + carry).astype(BF16)
            carry = carry + cs[rs, LANES:]
        carry_scr[...] = carry
        acc_scr[...] += _dot_nt(jnp.concatenate(ws, axis=1), vg)
        return 0

    lax.fori_loop(0, ng, group, 0)
    acc = acc_scr[...]
    out = jnp.zeros((TQ, SB_W), F32)
    for h in range(SB_HEADS):
        out = out + jnp.where(lane_head == h, acc[h * TQ:(h + 1) * TQ, :], 0.0)
    o_ref[0] = out


def _sb_decode(q, k_new, v_new, bias, tail_mat, cache_k, cache_v, page_table, layer):
    db, tq, w = q.shape
    npages = page_table.shape[1]
    g = math.gcd(npages, PAGES_PER_GROUP)
    assert cache_k.shape[2:] == (w, SB_TILE)
    tok = pl.BlockSpec((1, tq, w), lambda b, pt: (b, 0, 0))
    grid_spec = pltpu.PrefetchScalarGridSpec(
        num_scalar_prefetch=1,
        grid=(db,),
        in_specs=[pl.BlockSpec(memory_space=pltpu.SMEM), tok, tok, tok,
                  pl.BlockSpec(tail_mat.shape, lambda b, pt: (0, 0)),
                  pl.BlockSpec(memory_space=pl.ANY), pl.BlockSpec(memory_space=pl.ANY)],
        out_specs=tok,
        scratch_shapes=[pltpu.VMEM((DECODE_SLOTS, w, g * SB_TILE), F32), pltpu.VMEM((DECODE_SLOTS, w, g * SB_TILE), F32),
                        pltpu.SemaphoreType.DMA((2, DECODE_SLOTS)),
                        pltpu.VMEM((SB_TILE, w), F32), pltpu.VMEM((SB_TILE, w), F32),
                        pltpu.VMEM((SB_HEADS * tq, LANES), F32), pltpu.VMEM((SB_HEADS * tq, w), F32)],
    )
    return pl.pallas_call(
        functools.partial(_sb_decode_body, TQ=tq, NP=npages, G=g, NB=db, NSLOT=DECODE_SLOTS,
                          LAYER=layer),
        grid_spec=grid_spec,
        out_shape=jax.ShapeDtypeStruct((db, tq, w), F32),
        compiler_params=_cparams(("arbitrary",)),
        name="sb_decode",
    )(page_table, bias, q, k_new, v_new, tail_mat, cache_k, cache_v)


def _conv_body(u_ref, prev_ref, buf_ref, w_ref, cb_ref, lg_ref, lb_ref, y_ref, bufo_ref, win_scr, *, RT, NT):
    i = pl.program_id(1)
    hist = CONV_WIDTH - 1
    first = HALO - hist

    @pl.when(i == 0)
    def _():
        win_scr[first:HALO, :] = buf_ref[0]

    if NT > 1:
        @pl.when(i > 0)
        def _():
            win_scr[0:HALO, :] = prev_ref[0]

    win_scr[HALO:HALO + RT, :] = u_ref[0]
    st = min(RT, CONV_SUBTILE)
    n = HALO + st
    for base in range(0, RT, st):
        win = win_scr[base:base + n, :]
        acc = jnp.zeros((st, CONV_CH), F32)
        for r in range(SUBLANES):
            offs = [o for o in range(first, first + CONV_WIDTH) if o % SUBLANES == r]
            shifted = win if r == 0 else pltpu.roll(win, n - r, 0)
            for o in offs:
                acc = acc + shifted[o - r:o - r + st, :] * w_ref[o - first:o - first + 1, :]
        y = acc + cb_ref[...]
        mu = jnp.mean(y, axis=-1, keepdims=True)
        yc = y - mu
        yn = yc * lax.rsqrt(jnp.mean(yc * yc, axis=-1, keepdims=True) + EPS) * lg_ref[...] + lb_ref[...]
        y_ref[0, base:base + st, :] = yn * _sigmoid(yn)

    @pl.when(i == pl.num_programs(1) - 1)
    def _():
        bufo_ref[0] = win_scr[first + RT:HALO + RT, :]


def _conv(u, buf, w, cb, lg, lb):
    bsz, t, ch = u.shape
    hist = CONV_WIDTH - 1
    rt = min(t, CONV_TILE)
    nt = t // rt
    assert t % rt == 0 and (nt == 1 or rt % HALO == 0)
    per = rt // HALO if nt > 1 else 1
    tile = pl.BlockSpec((1, rt, ch), lambda b, i: (b, i, 0))
    prev_rows = HALO if nt > 1 else rt
    prev = pl.BlockSpec((1, prev_rows, ch), lambda b, i: (b, jnp.maximum(i * per - 1, 0), 0))
    state = pl.BlockSpec((1, hist, ch), lambda b, i: (b, 0, 0))
    vec = lambda r: pl.BlockSpec((r, ch), lambda b, i: (0, 0))
    return pl.pallas_call(
        functools.partial(_conv_body, RT=rt, NT=nt),
        grid=(bsz, nt),
        in_specs=[tile, prev, state, vec(CONV_WIDTH), vec(1), vec(1), vec(1)],
        out_specs=[tile, state],
        out_shape=[jax.ShapeDtypeStruct((bsz, t, ch), F32), jax.ShapeDtypeStruct((bsz, hist, ch), F32)],
        scratch_shapes=[pltpu.VMEM((HALO + rt, ch), F32)],
        compiler_params=_cparams(("parallel", "arbitrary")),
        name="conv_branch",
    )(u, u, buf, w, cb, lg, lb)


def _layer_weights(l, w):
    d = w["w_in"].shape[1]
    w_in = w["w_in"][l]
    sizes = (GLA_QK, GLA_QK, GLA_V, GLA_V, GLA_RANK, SB_W, SB_W, SB_W, 2 * CONV_CH)
    offs = [0]
    for s in sizes:
        offs.append(offs[-1] + s)
    main = jnp.concatenate([w_in[:, :offs[4]], w_in[:, offs[5]:]], axis=1).astype(BF16)
    w_lr = jnp.zeros((d, LANES), F32).at[:, :GLA_RANK].set(w_in[:, offs[4]:offs[5]]).astype(BF16)
    w_a2 = jnp.zeros((LANES, GLA_QK), F32).at[:GLA_RANK, :].set(w["gla_w_a2"][l]).astype(BF16)
    row = lambda name: w[name][l][None, :]
    return dict(
        layer=l,
        ffn1=(w["ffn1_norm_pre"][:, None, :], w["ffn1_norm_post"][:, None, :], w["ffn1_w_gate_bf"], w["ffn1_w_up_bf"], w["ffn1_w_down_bf"]),
        ffn2=(w["ffn2_norm_pre"][:, None, :], w["ffn2_norm_post"][:, None, :], w["ffn2_w_gate_bf"], w["ffn2_w_up_bf"], w["ffn2_w_down_bf"]),
        mixin=(row("mix_norm_pre"), main, w_lr, w_a2, row("gla_b_a"), w_in[:, offs[6]:offs[8]].T.astype(BF16)),
        gla_norm=row("gla_norm"),
        sb_bias=w["sb_bias"][l],
        sb_bias_rows=jnp.broadcast_to(jnp.repeat(w["sb_bias"][l], SB_TILE)[:, None], (SB_HEADS * SB_TILE, LANES)),
        conv=(w["conv_w"][l], row("conv_b"), row("conv_norm_g"), row("conv_norm_b")),
        w_out=w["w_out_bf"],
        mix_post=w["mix_norm_post"][:, None, :],
    )


def _token_tile(n):
    for tm in (1024, 512, 256, 128, 64, 32, 16, 8):
        if n % tm == 0:
            return tm
    raise ValueError(n)


def _ffn_tile(f):
    for tf in (256, 128):
        if f % tf == 0:
            return tf
    return f


def _group_layer(x, lw, s0, buf, tail_mat, attend, kv_t):
    bsz, t, d = x.shape
    n = bsz * t
    tm = _token_tile(n)
    tf = _ffn_tile(lw["ffn1"][2].shape[2])
    x2 = _ffn(x.reshape(n, d), *lw["ffn1"], lw["layer"], tm=tm, tf=tf)
    gq, gk, gv, gg, la, sq, sk, sv, u = _mixin(x2, *lw["mixin"], tm=min(tm, 512, t) if kv_t else min(tm, 512),
                                               seq_len=t, kv_t=kv_t)
    seq = lambda a: a.reshape(bsz, t, a.shape[-1])
    o_gla, st = _gla(seq(gq), seq(gk), seq(gv), seq(la), seq(gg), _state_to_t(s0), lw["gla_norm"])
    o_sb = attend(seq(sq), sk, sv) if kv_t else attend(seq(sq), seq(sk), seq(sv))
    y, buf_new = _conv(seq(u), buf, *lw["conv"])
    mix = (o_gla.reshape(n, GLA_V), o_sb.reshape(n, SB_W), y.reshape(n, CONV_CH), lw["w_out"], lw["mix_post"])
    x4 = _ffn(x2, *lw["ffn2"], lw["layer"], tm=tm, tf=tf, mix=mix)
    if kv_t:
        heads = lambda a: a.reshape(bsz, SB_HEADS, SB_DIM, t)
    else:
        heads = lambda a: a.reshape(bsz, t, SB_HEADS, SB_DIM)
    return x4.reshape(bsz, t, d), heads(sk), heads(sv), _state_from_t(st), buf_new


def kernel(x_prompt, x_sample, cache_sb_k, cache_sb_v, page_table, state_gla, state_conv, ffn1_norm_pre, ffn1_norm_post, ffn1_w_gate, ffn1_w_up, ffn1_w_down, mix_norm_pre, mix_norm_post, w_in, gla_w_a2, gla_b_a, gla_norm, sb_bias, conv_w, conv_b, conv_norm_g, conv_norm_b, w_out, ffn2_norm_pre, ffn2_norm_post, ffn2_w_gate, ffn2_w_up, ffn2_w_down):
    weights = dict(ffn1_norm_pre=ffn1_norm_pre, ffn1_norm_post=ffn1_norm_post, ffn1_w_gate=ffn1_w_gate,
                   ffn1_w_up=ffn1_w_up, ffn1_w_down=ffn1_w_down, mix_norm_pre=mix_norm_pre,
                   mix_norm_post=mix_norm_post, w_in=w_in, gla_w_a2=gla_w_a2, gla_b_a=gla_b_a, gla_norm=gla_norm,
                   sb_bias=sb_bias, conv_w=conv_w, conv_b=conv_b, conv_norm_g=conv_norm_g,
                   conv_norm_b=conv_norm_b, w_out=w_out, ffn2_norm_pre=ffn2_norm_pre,
                   ffn2_norm_post=ffn2_norm_post, ffn2_w_gate=ffn2_w_gate, ffn2_w_up=ffn2_w_up,
                   ffn2_w_down=ffn2_w_down)
    for name in ("ffn1_w_gate", "ffn1_w_up", "ffn1_w_down", "ffn2_w_gate", "ffn2_w_up", "ffn2_w_down", "w_out"):
        weights[name + "_bf"] = weights[name].astype(BF16)
    depth = w_in.shape[0]
    bsz = x_prompt.shape[0]
    n_pool, page = cache_sb_k.shape[1], cache_sb_k.shape[2]
    tail_mat = _tail_matrix()
    ck = jnp.transpose(cache_sb_k, (0, 1, 3, 4, 2)).reshape(depth, n_pool, SB_W, page)
    cv = jnp.transpose(cache_sb_v, (0, 1, 3, 4, 2)).reshape(depth, n_pool, SB_W, page)
    xp, xs = x_prompt, x_sample
    outs = [[] for _ in range(8)]
    for l in range(depth):
        lw = _layer_weights(l, weights)
        s0 = jnp.zeros((bsz, GLA_HEADS, GLA_DK, GLA_DV), F32)
        buf0 = jnp.zeros((bsz, CONV_WIDTH - 1, CONV_CH), F32)
        prompt_attend = lambda q, k, v: _sb_prompt(q, k, v, lw["sb_bias_rows"], tail_mat)
        xp, k1, v1, s1, c1 = _group_layer(xp, lw, s0, buf0, tail_mat, prompt_attend, True)
        decode_attend = lambda q, k, v, l=l: _sb_decode(q, k, v, lw["sb_bias"], tail_mat, ck, cv, page_table, l)
        xs, k2, v2, s2, c2 = _group_layer(xs, lw, state_gla[l], state_conv[l], tail_mat, decode_attend, False)
        for lst, val in zip(outs, (k1, v1, s1, c1, k2, v2, s2, c2)):
            lst.append(val)
    stacked = [jnp.stack(o) for o in outs]
    for idx in (0, 1):
        stacked[idx] = jnp.transpose(stacked[idx], (0, 1, 4, 2, 3))
    return (xp, xs) + tuple(stacked)
```

```python
import functools
import math

import jax
import jax.numpy as jnp
from jax import lax
from jax.experimental import pallas as pl
from jax.experimental.pallas import tpu as pltpu

F32 = jnp.float32
BF16 = jnp.bfloat16

EPS = 1e-6
GLA_HEADS = 4
GLA_DK = 64
GLA_DV = 128
GLA_RANK = 16
GLA_TAU = 16.0
GLA_CHUNK = 128
SB_HEADS = 4
SB_DIM = 64
CONV_CH = 256
CONV_WIDTH = 31
GLA_QK = GLA_HEADS * GLA_DK
GLA_V = GLA_HEADS * GLA_DV
SB_W = SB_HEADS * SB_DIM

LANES = 128
SUBLANES = 8
SB_TILE = 128
SB_TILES_PER_TRIP = 4
HALO = 32
CONV_TILE = 256
CONV_SUBTILE = 64
GLA_SAFE_DECAY = 60.0
GLA_SEQS_PER_STEP = 4
PAGES_PER_GROUP = 16
DECODE_SLOTS = 4
VMEM_LIMIT = 48 * 1024 * 1024
FFN_VMEM_LIMIT = 56 * 1024 * 1024


def _cparams(sem):
    return pltpu.CompilerParams(dimension_semantics=sem, vmem_limit_bytes=VMEM_LIMIT)


def _rms(x, g):
    return x * lax.rsqrt(jnp.mean(x * x, axis=-1, keepdims=True) + EPS) * g


def _sigmoid(x):
    return 1.0 / (1.0 + jnp.exp(-x))


def _log_sigmoid(x):
    return jnp.minimum(x, 0.0) - jnp.log(1.0 + jnp.exp(-jnp.abs(x)))


def _split2(x):
    hi = x.astype(BF16)
    lo = (x - hi.astype(F32)).astype(BF16)
    return hi, lo


def _split3(x):
    hi = x.astype(BF16)
    r = x - hi.astype(F32)
    mid = r.astype(BF16)
    lo = (r - mid.astype(F32)).astype(BF16)
    return hi, mid, lo


def _dot(a, b):
    return jnp.dot(a, b, preferred_element_type=F32)


def _dot_nt(a, b):
    return lax.dot_general(a, b, (((1,), (1,)), ((), ())), preferred_element_type=F32)


def _dot_tn(a, b):
    return lax.dot_general(a, b, (((0,), (0,)), ((), ())), preferred_element_type=F32)


def _ffn_body(*refs, TF, MIX):
    if MIX:
        x_in_ref, og_ref, osb_ref, y_ref, wout_ref, gmix_ref = refs[:6]
        gpre_ref, gpost_ref, wg_ref, wu_ref, wd_ref, o_ref, h_scr, acc_scr = refs[6:]
        m = _dot(og_ref[...].astype(BF16), wout_ref[0:GLA_V, :])
        m = m + _dot(osb_ref[...].astype(BF16), wout_ref[GLA_V:GLA_V + SB_W, :])
        m = m + _dot(y_ref[...].astype(BF16), wout_ref[GLA_V + SB_W:, :])
        o_ref[...] = x_in_ref[...] + _rms(m, gmix_ref[...])
        x_ref = o_ref
    else:
        x_ref, gpre_ref, gpost_ref, wg_ref, wu_ref, wd_ref, o_ref, h_scr, acc_scr = refs
    h_scr[...] = _rms(x_ref[...], gpre_ref[...]).astype(BF16)
    nf = wg_ref.shape[1] // TF

    def chunk(f):
        cols = pl.ds(pl.multiple_of(f * TF, TF), TF)
        h = h_scr[...]
        g = _dot(h, wg_ref[:, cols])
        u = _dot(h, wu_ref[:, cols])
        a = (g * _sigmoid(g)) * u
        return _dot(a.astype(BF16), wd_ref[cols, :])

    acc_scr[...] = chunk(0)

    def body(f, carry):
        acc_scr[...] += chunk(f)
        return carry

    lax.fori_loop(1, nf, body, 0, unroll=2 if (nf - 1) % 2 == 0 else 1)
    o_ref[...] = x_ref[...] + 0.5 * _rms(acc_scr[...], gpost_ref[...])


def _ffn_stream_body(*refs, MIX):
    if MIX:
        x_in_ref, og_ref, osb_ref, y_ref, wout_ref, gmix_ref = refs[:6]
        refs = refs[6:]
    else:
        x_in_ref, refs = refs[0], refs[1:]
    gpre_ref, gpost_ref, wg_ref, wu_ref, wd_ref, o_ref, wg_out, wu_out, wd_out, h_scr, acc_scr = refs
    f = pl.program_id(1)

    @pl.when(f == 0)
    def _():
        x = x_in_ref[...]
        if MIX:
            m = _dot(og_ref[...].astype(BF16), wout_ref[0:GLA_V, :])
            m = m + _dot(osb_ref[...].astype(BF16), wout_ref[GLA_V:GLA_V + SB_W, :])
            m = m + _dot(y_ref[...].astype(BF16), wout_ref[GLA_V + SB_W:, :])
            x = x + _rms(m, gmix_ref[...])
        o_ref[...] = x
        h_scr[...] = _rms(x, gpre_ref[...]).astype(BF16)
        acc_scr[...] = jnp.zeros_like(acc_scr)

    wg, wu, wd = wg_ref[...].astype(BF16), wu_ref[...].astype(BF16), wd_ref[...].astype(BF16)
    wg_out[...] = wg
    wu_out[...] = wu
    wd_out[...] = wd
    h = h_scr[...]
    g = _dot(h, wg)
    u = _dot(h, wu)
    acc_scr[...] += _dot(((g * _sigmoid(g)) * u).astype(BF16), wd)

    @pl.when(f == pl.num_programs(1) - 1)
    def _():
        o_ref[...] = o_ref[...] + 0.5 * _rms(acc_scr[...], gpost_ref[...])


def _ffn_stream(x, g_pre, g_post, w_gate, w_up, w_down, layer, *, tm, tf, mix=None):
    n, d = x.shape
    hidden = w_gate.shape[2]
    const = lambda a: pl.BlockSpec((None,) + a.shape[1:], lambda i, f: (layer, 0, 0))
    gain = pl.BlockSpec((None, 1, d), lambda i, f: (layer, 0, 0))
    tok = lambda w: pl.BlockSpec((tm, w), lambda i, f: (i, 0))
    in_specs, args = [tok(d)], [x]
    if mix is not None:
        og, osb, y, w_out, g_mix = mix
        in_specs += [tok(GLA_V), tok(SB_W), tok(CONV_CH), const(w_out), gain]
        args += [og, osb, y, w_out, g_mix]
    in_specs += [gain, gain,
                 pl.BlockSpec((None, d, tf), lambda i, f: (layer, 0, f)),
                 pl.BlockSpec((None, d, tf), lambda i, f: (layer, 0, f)),
                 pl.BlockSpec((None, tf, d), lambda i, f: (layer, f, 0))]
    args += [g_pre, g_post, w_gate, w_up, w_down]
    return pl.pallas_call(
        functools.partial(_ffn_stream_body, MIX=mix is not None),
        grid=(n // tm, hidden // tf),
        in_specs=in_specs,
        out_specs=[tok(d), pl.BlockSpec((d, tf), lambda i, f: (0, f)), pl.BlockSpec((d, tf), lambda i, f: (0, f)),
                   pl.BlockSpec((tf, d), lambda i, f: (f, 0))],
        out_shape=[jax.ShapeDtypeStruct((n, d), F32), jax.ShapeDtypeStruct((d, hidden), BF16),
                   jax.ShapeDtypeStruct((d, hidden), BF16), jax.ShapeDtypeStruct((hidden, d), BF16)],
        scratch_shapes=[pltpu.VMEM((tm, d), BF16), pltpu.VMEM((tm, d), F32)],
        compiler_params=_cparams(("arbitrary", "arbitrary")),
        name="half_ffn_stream" if mix is None else "mixer_out_ffn_stream",
    )(*args)


def _ffn(x, g_pre, g_post, w_gate, w_up, w_down, layer, *, tm, tf, mix=None):
    n, d = x.shape
    resident = lambda a: pl.BlockSpec(a.shape, lambda i: (0, 0), pipeline_mode=pl.Buffered(1))
    gain = pl.BlockSpec((None, 1, d), lambda i: (layer, 0, 0))
    tok = lambda w: pl.BlockSpec((tm, w), lambda i: (i, 0))
    in_specs, args = [tok(d)], [x]
    if mix is not None:
        og, osb, y, w_out, g_mix = mix
        in_specs += [tok(GLA_V), tok(SB_W), tok(CONV_CH),
                     pl.BlockSpec((None,) + w_out.shape[1:], lambda i: (layer, 0, 0), pipeline_mode=pl.Buffered(1)),
                     gain]
        args += [og, osb, y, w_out, g_mix]
    in_specs += [gain, gain, resident(w_gate), resident(w_up), resident(w_down)]
    args += [g_pre, g_post, w_gate, w_up, w_down]
    return pl.pallas_call(
        functools.partial(_ffn_body, TF=tf, MIX=mix is not None),
        grid=(n // tm,),
        in_specs=in_specs,
        out_specs=tok(d),
        out_shape=jax.ShapeDtypeStruct((n, d), F32),
        scratch_shapes=[pltpu.VMEM((tm, d), BF16), pltpu.VMEM((tm, d), F32)],
        compiler_params=pltpu.CompilerParams(dimension_semantics=("parallel",),
                                             vmem_limit_bytes=FFN_VMEM_LIMIT if mix is not None else VMEM_LIMIT),
        name="half_ffn" if mix is None else "mixer_out_ffn",
    )(*args)


_SEG = {}
_off = 0
for _name, _w in (("gq", GLA_QK), ("gk", GLA_QK), ("gv", GLA_V), ("gg", GLA_V), ("sq", SB_W), ("sk", SB_W),
                  ("sv", SB_W), ("ca", CONV_CH), ("cg", CONV_CH)):
    _SEG[_name] = (_off, _off + _w)
    _off += _w
MAIN_COLS = _off


def _mixin_body(x_ref, g_ref, w_ref, wlr_ref, wa2_ref, ba_ref, *rest, KV_T):
    if KV_T:
        wkvt_ref, rest = rest[0], rest[1:]
    gq_ref, gk_ref, gv_ref, gg_ref, la_ref, sq_ref, sk_ref, sv_ref, u_ref = rest
    h = _rms(x_ref[...], g_ref[...]).astype(BF16)

    def seg(name):
        lo, hi = _SEG[name]
        return _dot(h, w_ref[:, lo:hi])

    gq_ref[...] = seg("gq") * (GLA_DK ** -0.5)
    gk_ref[...] = seg("gk")
    gv_ref[...] = seg("gv")
    gg_ref[...] = seg("gg")
    sq_ref[...] = seg("sq")
    if KV_T:
        sk_ref[0] = _dot_nt(wkvt_ref[0:SB_W, :], h)
        sv_ref[0] = _dot_nt(wkvt_ref[SB_W:2 * SB_W, :], h)
    else:
        sk_ref[...] = seg("sk")
        sv_ref[...] = seg("sv")
    u_ref[...] = seg("ca") * _sigmoid(seg("cg"))
    lr = _dot(h, wlr_ref[...])
    xa = _dot(lr.astype(BF16), wa2_ref[...]) + ba_ref[...]
    la_ref[...] = _log_sigmoid(xa) * (1.0 / GLA_TAU)


def _mixin(x, g, w_main, w_lr, w_a2, b_a, w_kvt, *, tm, seq_len, kv_t):
    n, d = x.shape
    widths = (GLA_QK, GLA_QK, GLA_V, GLA_V, GLA_QK, SB_W, SB_W, SB_W, CONV_CH)
    const = lambda shape: pl.BlockSpec(shape, lambda i: (0, 0))
    in_specs = [pl.BlockSpec((tm, d), lambda i: (i, 0)), const((1, d)), const(w_main.shape),
                const(w_lr.shape), const(w_a2.shape), const((1, GLA_QK))]
    out_specs = [pl.BlockSpec((tm, w), lambda i: (i, 0)) for w in widths]
    out_shape = [jax.ShapeDtypeStruct((n, w), F32) for w in widths]
    args = [x, g, w_main, w_lr, w_a2, b_a]
    if kv_t:
        assert seq_len % tm == 0
        per_seq = seq_len // tm
        in_specs.append(const(w_kvt.shape))
        args.append(w_kvt)
        for idx in (6, 7):
            out_specs[idx] = pl.BlockSpec((1, SB_W, tm), lambda i: (i // per_seq, 0, i % per_seq))
            out_shape[idx] = jax.ShapeDtypeStruct((n // seq_len, SB_W, seq_len), F32)
    return pl.pallas_call(
        functools.partial(_mixin_body, KV_T=kv_t),
        grid=(n // tm,),
        in_specs=in_specs,
        out_specs=out_specs,
        out_shape=out_shape,
        compiler_params=_cparams(("parallel",)),
        name="mixer_in",
    )(*args)


def _gla_body(q_ref, k_ref, v_ref, la_ref, gg_ref, s0_ref, gn_ref, o_ref, sout_ref, st_scr, b_scr, oi_scr, qbs_scr,
              *, C, NB):
    c = pl.program_id(1)

    @pl.when(c == 0)
    def _():
        st_scr[...] = s0_ref[...]

    row = lax.broadcasted_iota(jnp.int32, (C, C), 0)
    col = lax.broadcasted_iota(jnp.int32, (C, C), 1)
    causal = row >= col
    tri = jnp.where(causal, 1.0, 0.0).astype(BF16)
    lane_head = lax.broadcasted_iota(jnp.int32, (C, GLA_QK), 1) // GLA_DK
    heads = [slice(h * GLA_DV, (h + 1) * GLA_DV) for h in range(GLA_HEADS)]

    def head_stack(x):
        return jnp.concatenate([jnp.where(lane_head == h, x, 0.0) for h in range(GLA_HEADS)], axis=0).astype(BF16)

    rows_t = lax.broadcasted_iota(jnp.int32, (GLA_HEADS * C, C), 0) % C
    causal_rows = rows_t >= lax.broadcasted_iota(jnp.int32, (GLA_HEADS * C, C), 1)

    decays = []
    for n in range(NB):
        q, k, v = q_ref[n], k_ref[n], v_ref[n]
        hi, mid, lo = _split3(la_ref[n])
        b = _dot(tri, jnp.concatenate([hi, mid, lo], axis=1))
        b = b[:, :GLA_QK] + b[:, GLA_QK:2 * GLA_QK] + b[:, 2 * GLA_QK:]
        b_scr[n] = b
        b_last = b[C - 1:C, :]
        decays.append(jnp.max(-b_last))
        st = st_scr[n]
        qb = q * jnp.exp(b)
        kd = k * jnp.exp(b_last - b)
        qbs = head_stack(qb)
        qbs_scr[n] = qbs
        oi = _dot_nt(qbs, st.astype(BF16))
        for h, sl in enumerate(heads):
            oi_scr[n, :, sl] = oi[h * C:(h + 1) * C]
        v_rows = jnp.concatenate([v[:, sl] for sl in heads], axis=0).astype(BF16)
        st_scr[n] = st * jnp.exp(b_last) + _dot_tn(v_rows, head_stack(kd))
    total_decay = functools.reduce(jnp.maximum, decays)

    @pl.when(total_decay <= GLA_SAFE_DECAY)
    def _():
        for n in range(NB):
            k, v, b = k_ref[n], v_ref[n], b_scr[n]
            kb = (k * jnp.exp(-b)).astype(BF16)
            attn = jnp.where(causal_rows, _dot_nt(qbs_scr[n], kb), 0.0).astype(BF16)
            for h, sl in enumerate(heads):
                oi_scr[n, :, sl] += _dot(attn[h * C:(h + 1) * C], v[:, sl].astype(BF16))

    @pl.when(total_decay > GLA_SAFE_DECAY)
    def _():
        rhead = lax.broadcasted_iota(jnp.int32, (GLA_QK, GLA_V), 0) // GLA_DK
        chead = lax.broadcasted_iota(jnp.int32, (GLA_QK, GLA_V), 1) // GLA_DV
        spread = jnp.where(rhead == chead, 1.0, 0.0).astype(BF16)
        trow = lax.broadcasted_iota(jnp.int32, (C, GLA_QK), 0)
        for n in range(NB):
            q, b = q_ref[n], b_scr[n]

            def body(s, acc):
                bs = b_scr[n, pl.ds(s, 1), :]
                ks = k_ref[n, pl.ds(s, 1), :]
                vs = v_ref[n, pl.ds(s, 1), :]
                p = q * ks * jnp.exp(jnp.minimum(b - bs, 0.0))
                p = jnp.where(trow >= s, p, 0.0)
                p_hi, p_mid, p_lo = _split3(p)
                r = _dot(p_hi, spread) + _dot(p_mid, spread) + _dot(p_lo, spread)
                return acc + r * vs

            oi_scr[n] += lax.fori_loop(0, C, body, jnp.zeros((C, GLA_V), F32))

    for n in range(NB):
        o, gg = oi_scr[n], gg_ref[n]
        for sl in heads:
            o_ref[n, :, sl] = _rms(o[:, sl], gn_ref[:, sl]) * (gg[:, sl] * _sigmoid(gg[:, sl]))

    @pl.when(c == pl.num_programs(1) - 1)
    def _():
        sout_ref[...] = st_scr[...]


def _gla(q, k, v, la, gg, s0_t, gnorm):
    bsz, t_real, _ = q.shape
    if t_real < 16:
        padt = lambda a: jnp.pad(a, ((0, 0), (0, 16 - t_real), (0, 0)))
        q, k, v, la, gg = (padt(a) for a in (q, k, v, la, gg))
    t = q.shape[1]
    c = min(GLA_CHUNK, t)
    nb = GLA_SEQS_PER_STEP if bsz % GLA_SEQS_PER_STEP == 0 else 1
    assert t % c == 0
    tok = lambda w: pl.BlockSpec((nb, c, w), lambda b, i: (b, i, 0))
    st = pl.BlockSpec((nb, GLA_DV, GLA_QK), lambda b, i: (b, 0, 0))
    o, st_new = pl.pallas_call(
        functools.partial(_gla_body, C=c, NB=nb),
        grid=(bsz // nb, t // c),
        in_specs=[tok(GLA_QK), tok(GLA_QK), tok(GLA_V), tok(GLA_QK), tok(GLA_V), st,
                  pl.BlockSpec((1, GLA_V), lambda b, i: (0, 0))],
        out_specs=[tok(GLA_V), st],
        out_shape=[jax.ShapeDtypeStruct((bsz, t, GLA_V), F32), jax.ShapeDtypeStruct((bsz, GLA_DV, GLA_QK), F32)],
        scratch_shapes=[pltpu.VMEM((nb, GLA_DV, GLA_QK), F32), pltpu.VMEM((nb, c, GLA_QK), F32),
                        pltpu.VMEM((nb, c, GLA_V), F32), pltpu.VMEM((nb, GLA_HEADS * c, GLA_QK), BF16)],
        compiler_params=_cparams(("parallel", "arbitrary")),
        name="gla",
    )(q, k, v, la, gg, s0_t, gnorm)
    return o[:, :t_real], st_new


def _state_to_t(s):
    b = s.shape[0]
    return jnp.transpose(s, (0, 3, 1, 2)).reshape(b, GLA_DV, GLA_QK)


def _state_from_t(st):
    b = st.shape[0]
    return jnp.transpose(st.reshape(b, GLA_DV, GLA_HEADS, GLA_DK), (0, 2, 3, 1))


def _sb_weights(z, tail_mat_ref, carry, mask):
    ls = _log_sigmoid(z)
    l1m = ls - z
    if mask is not None:
        l1m = jnp.where(mask, l1m, 0.0)
    hi, lo = _split2(l1m)
    cs = _dot(jnp.concatenate([hi, lo], axis=1), tail_mat_ref[...])
    w = jnp.exp(ls + cs[:, :LANES] + carry)
    if mask is not None:
        w = jnp.where(mask, w, 0.0)
    return w, carry + cs[:, LANES:]


def _tail_matrix():
    src = jnp.arange(2 * LANES)[:, None] % LANES
    dst = jnp.arange(2 * LANES)[None, :]
    return jnp.where((dst >= LANES) | (src > dst), 1.0, 0.0).astype(BF16)


def _sb_prompt_body(bias_ref, q_ref, k_ref, v_ref, tm_ref, o_ref, kbf_scr, vbd_scr, qbd_scr, carry_scr, acc_scr,
                    *, T):
    i = pl.program_id(1)
    nkb = T // SB_TILE
    R = SB_HEADS * SB_TILE
    lane_head = lax.broadcasted_iota(jnp.int32, (SB_TILE, SB_W), 1) // SB_DIM

    @pl.when(i == 0)
    def _():
        kbf_scr[...] = k_ref[0].astype(BF16)
        for jb in range(nkb):
            vb = v_ref[0, :, jb * SB_TILE:(jb + 1) * SB_TILE].T
            for h in range(SB_HEADS):
                vbd_scr[jb, h * SB_TILE:(h + 1) * SB_TILE, :] = jnp.where(lane_head == h, vb, 0.0).astype(BF16)

    q = q_ref[0] * (SB_DIM ** -0.5)
    for h in range(SB_HEADS):
        qbd_scr[h * SB_TILE:(h + 1) * SB_TILE, :] = jnp.where(lane_head == h, q, 0.0).astype(BF16)
    carry_scr[...] = jnp.zeros_like(carry_scr)
    acc_scr[...] = jnp.zeros_like(acc_scr)

    def tile(j, mask):
        kblk = kbf_scr[:, pl.ds(pl.multiple_of(j * SB_TILE, SB_TILE), SB_TILE)]
        z = _dot(qbd_scr[...], kblk) + bias_ref[...]
        w, carry_scr[...] = _sb_weights(z, tm_ref, carry_scr[...], mask)
        wb = w.astype(BF16)
        wcat = jnp.concatenate([wb[h * SB_TILE:(h + 1) * SB_TILE] for h in range(SB_HEADS)], axis=1)
        acc_scr[...] += _dot(wcat, vbd_scr[j])

    trow = lax.broadcasted_iota(jnp.int32, (R, SB_TILE), 0) % SB_TILE
    scol = lax.broadcasted_iota(jnp.int32, (R, SB_TILE), 1)
    rem = i % SB_TILES_PER_TRIP
    for r in range(SB_TILES_PER_TRIP):
        @pl.when(rem == r)
        def _(r=r):
            tile(i, scol < trow)
            for u in range(r):
                tile(i - 1 - u, None)

    def group(jj, _):
        for u in range(SB_TILES_PER_TRIP):
            tile(i - rem - 1 - u - SB_TILES_PER_TRIP * jj, None)
        return 0

    lax.fori_loop(0, i // SB_TILES_PER_TRIP, group, 0)

    o_ref[0] = acc_scr[...]


def _sb_prompt(q, k_t, v_t, bias_rows, tail_mat):
    bsz, t, w = q.shape
    nq = t // SB_TILE
    full = pl.BlockSpec((1, w, t), lambda b, i: (b, 0, 0))
    qt = pl.BlockSpec((1, SB_TILE, w), lambda b, i: (b, i, 0))
    const = lambda a: pl.BlockSpec(a.shape, lambda b, i: (0, 0))
    return pl.pallas_call(
        functools.partial(_sb_prompt_body, T=t),
        grid=(bsz, nq),
        in_specs=[const(bias_rows), qt, full, full, const(tail_mat)],
        out_specs=qt,
        out_shape=jax.ShapeDtypeStruct((bsz, t, w), F32),
        scratch_shapes=[pltpu.VMEM((w, t), BF16), pltpu.VMEM((nq, SB_HEADS * SB_TILE, w), BF16),
                        pltpu.VMEM((SB_HEADS * SB_TILE, w), BF16), pltpu.VMEM((SB_HEADS * SB_TILE, LANES), F32),
                        pltpu.VMEM((SB_TILE, w), F32)],
        compiler_params=_cparams(("parallel", "arbitrary")),
        name="sb_prompt",
    )(bias_rows, q, k_t, v_t, tail_mat)


def _sb_decode_body(pt_ref, bias_ref, q_ref, kn_ref, vn_ref, tm_ref, kc_hbm, vc_hbm, o_ref,
                    kbuf, vbuf, sem, knew_scr, vnew_scr, carry_scr, acc_scr, *, TQ, NP, G, NB, NSLOT, LAYER):
    b = pl.program_id(0)
    nb = NB
    ng = NP // G
    R = SB_HEADS * TQ

    def copies(bb, grp, slot):
        out = []
        for p in range(G):
            page = pt_ref[bb, grp * G + p]
            keys = pl.ds(p * SB_TILE, SB_TILE)
            out.append(pltpu.make_async_copy(kc_hbm.at[LAYER, page], kbuf.at[slot, :, keys], sem.at[0, slot]))
            out.append(pltpu.make_async_copy(vc_hbm.at[LAYER, page], vbuf.at[slot, :, keys], sem.at[1, slot]))
        return out

    def start_group(step):
        for cp in copies(step // ng, ng - 1 - step % ng, step % NSLOT):
            cp.start()

    @pl.when(b == 0)
    def _():
        for s in range(min(NSLOT - 1, nb * ng)):
            start_group(s)

    lane_head = lax.broadcasted_iota(jnp.int32, (TQ, SB_W), 1) // SB_DIM
    q = q_ref[0] * (SB_DIM ** -0.5)
    qbd = jnp.concatenate([jnp.where(lane_head == h, q, 0.0) for h in range(SB_HEADS)], axis=0).astype(BF16)
    bias = jnp.concatenate([jnp.full((TQ, LANES), bias_ref[h], F32) for h in range(SB_HEADS)], axis=0)

    knew_scr[...] = jnp.zeros_like(knew_scr)
    vnew_scr[...] = jnp.zeros_like(vnew_scr)
    knew_scr[0:TQ, :] = kn_ref[0]
    vnew_scr[0:TQ, :] = vn_ref[0]
    trow = lax.broadcasted_iota(jnp.int32, (R, LANES), 0) % TQ
    scol = lax.broadcasted_iota(jnp.int32, (R, LANES), 1)
    z = _dot_nt(qbd, knew_scr[...].astype(BF16)) + bias
    w, carry = _sb_weights(z, tm_ref, jnp.zeros((R, LANES), F32), scol < trow)
    carry_scr[...] = carry
    acc_scr[...] = _dot(w.astype(BF16), vnew_scr[...].astype(BF16))

    def group(r, _):
        step = b * ng + r
        slot = step % NSLOT
        for cp in copies(b, ng - 1 - r, slot):
            cp.wait()

        @pl.when(step + NSLOT - 1 < nb * ng)
        def _():
            start_group(step + NSLOT - 1)

        kg = kbuf[slot].astype(BF16)
        vg = vbuf[slot].astype(BF16)
        zg = _dot(qbd, kg)
        zrows = jnp.concatenate([zg[:, p * LANES:(p + 1) * LANES] + bias for p in range(G)], axis=0)
        ls = _log_sigmoid(zrows)
        hi, lo = _split2(ls - zrows)
        cs = _dot(jnp.concatenate([hi, lo], axis=1), tm_ref[...])
        carry = carry_scr[...]
        ws = [None] * G
        for p in reversed(range(G)):
            rs = slice(p * R, (p + 1) * R)
            ws[p] = jnp.exp(ls[rs] + cs[rs, :LANES] + carry).astype(BF16)
            carry = carry + cs[rs, LANES:]
        carry_scr[...] = carry
        acc_scr[...] += _dot_nt(jnp.concatenate(ws, axis=1), vg)
        return 0

    lax.fori_loop(0, ng, group, 0)
    acc = acc_scr[...]
    out = jnp.zeros((TQ, SB_W), F32)
    for h in range(SB_HEADS):
        out = out + jnp.where(lane_head == h, acc[h * TQ:(h + 1) * TQ, :], 0.0)
    o_ref[0] = out


def _sb_decode(q, k_new, v_new, bias, tail_mat, cache_k, cache_v, page_table, layer):
    db, tq, w = q.shape
    npages = page_table.shape[1]
    g = math.gcd(npages, PAGES_PER_GROUP)
    assert cache_k.shape[2:] == (w, SB_TILE)
    tok = pl.BlockSpec((1, tq, w), lambda b, pt: (b, 0, 0))
    grid_spec = pltpu.PrefetchScalarGridSpec(
        num_scalar_prefetch=1,
        grid=(db,),
        in_specs=[pl.BlockSpec(memory_space=pltpu.SMEM), tok, tok, tok,
                  pl.BlockSpec(tail_mat.shape, lambda b, pt: (0, 0)),
                  pl.BlockSpec(memory_space=pl.ANY), pl.BlockSpec(memory_space=pl.ANY)],
        out_specs=tok,
        scratch_shapes=[pltpu.VMEM((DECODE_SLOTS, w, g * SB_TILE), F32), pltpu.VMEM((DECODE_SLOTS, w, g * SB_TILE), F32),
                        pltpu.SemaphoreType.DMA((2, DECODE_SLOTS)),
                        pltpu.VMEM((SB_TILE, w), F32), pltpu.VMEM((SB_TILE, w), F32),
                        pltpu.VMEM((SB_HEADS * tq, LANES), F32), pltpu.VMEM((SB_HEADS * tq, w), F32)],
    )
    return pl.pallas_call(
        functools.partial(_sb_decode_body, TQ=tq, NP=npages, G=g, NB=db, NSLOT=DECODE_SLOTS,
                          LAYER=layer),
        grid_spec=grid_spec,
        out_shape=jax.ShapeDtypeStruct((db, tq, w), F32),
        compiler_params=_cparams(("arbitrary",)),
        name="sb_decode",
    )(page_table, bias, q, k_new, v_new, tail_mat, cache_k, cache_v)


def _conv_body(u_ref, prev_ref, buf_ref, w_ref, cb_ref, lg_ref, lb_ref, y_ref, bufo_ref, win_scr, *, RT, NT):
    i = pl.program_id(1)
    hist = CONV_WIDTH - 1
    first = HALO - hist

    @pl.when(i == 0)
    def _():
        win_scr[first:HALO, :] = buf_ref[0]

    if NT > 1:
        @pl.when(i > 0)
        def _():
            win_scr[0:HALO, :] = prev_ref[0]

    win_scr[HALO:HALO + RT, :] = u_ref[0]
    st = min(RT, CONV_SUBTILE)
    n = HALO + st
    for base in range(0, RT, st):
        win = win_scr[base:base + n, :]
        acc = jnp.zeros((st, CONV_CH), F32)
        for r in range(SUBLANES):
            offs = [o for o in range(first, first + CONV_WIDTH) if o % SUBLANES == r]
            shifted = win if r == 0 else pltpu.roll(win, n - r, 0)
            for o in offs:
                acc = acc + shifted[o - r:o - r + st, :] * w_ref[o - first:o - first + 1, :]
        y = acc + cb_ref[...]
        mu = jnp.mean(y, axis=-1, keepdims=True)
        yc = y - mu
        yn = yc * lax.rsqrt(jnp.mean(yc * yc, axis=-1, keepdims=True) + EPS) * lg_ref[...] + lb_ref[...]
        y_ref[0, base:base + st, :] = yn * _sigmoid(yn)

    @pl.when(i == pl.num_programs(1) - 1)
    def _():
        bufo_ref[0] = win_scr[first + RT:HALO + RT, :]


def _conv(u, buf, w, cb, lg, lb):
    bsz, t, ch = u.shape
    hist = CONV_WIDTH - 1
    rt = min(t, CONV_TILE)
    nt = t // rt
    assert t % rt == 0 and (nt == 1 or rt % HALO == 0)
    per = rt // HALO if nt > 1 else 1
    tile = pl.BlockSpec((1, rt, ch), lambda b, i: (b, i, 0))
    prev_rows = HALO if nt > 1 else rt
    prev = pl.BlockSpec((1, prev_rows, ch), lambda b, i: (b, jnp.maximum(i * per - 1, 0), 0))
    state = pl.BlockSpec((1, hist, ch), lambda b, i: (b, 0, 0))
    vec = lambda r: pl.BlockSpec((r, ch), lambda b, i: (0, 0))
    return pl.pallas_call(
        functools.partial(_conv_body, RT=rt, NT=nt),
        grid=(bsz, nt),
        in_specs=[tile, prev, state, vec(CONV_WIDTH), vec(1), vec(1), vec(1)],
        out_specs=[tile, state],
        out_shape=[jax.ShapeDtypeStruct((bsz, t, ch), F32), jax.ShapeDtypeStruct((bsz, hist, ch), F32)],
        scratch_shapes=[pltpu.VMEM((HALO + rt, ch), F32)],
        compiler_params=_cparams(("parallel", "arbitrary")),
        name="conv_branch",
    )(u, u, buf, w, cb, lg, lb)


def _layer_weights(l, w):
    d = w["w_in"].shape[1]
    w_in = w["w_in"][l]
    sizes = (GLA_QK, GLA_QK, GLA_V, GLA_V, GLA_RANK, SB_W, SB_W, SB_W, 2 * CONV_CH)
    offs = [0]
    for s in sizes:
        offs.append(offs[-1] + s)
    main = jnp.concatenate([w_in[:, :offs[4]], w_in[:, offs[5]:]], axis=1).astype(BF16)
    w_lr = jnp.zeros((d, LANES), F32).at[:, :GLA_RANK].set(w_in[:, offs[4]:offs[5]]).astype(BF16)
    w_a2 = jnp.zeros((LANES, GLA_QK), F32).at[:GLA_RANK, :].set(w["gla_w_a2"][l]).astype(BF16)
    row = lambda name: w[name][l][None, :]
    return dict(
        layer=l,
        ffn1_gains=(w["ffn1_norm_pre"][:, None, :], w["ffn1_norm_post"][:, None, :]),
        ffn2_gains=(w["ffn2_norm_pre"][:, None, :], w["ffn2_norm_post"][:, None, :]),
        ffn1_f32=(w["ffn1_w_gate"], w["ffn1_w_up"], w["ffn1_w_down"]),
        ffn2_f32=(w["ffn2_w_gate"], w["ffn2_w_up"], w["ffn2_w_down"]),
        mixin=(row("mix_norm_pre"), main, w_lr, w_a2, row("gla_b_a"), w_in[:, offs[6]:offs[8]].T.astype(BF16)),
        gla_norm=row("gla_norm"),
        sb_bias=w["sb_bias"][l],
        sb_bias_rows=jnp.broadcast_to(jnp.repeat(w["sb_bias"][l], SB_TILE)[:, None], (SB_HEADS * SB_TILE, LANES)),
        conv=(w["conv_w"][l], row("conv_b"), row("conv_norm_g"), row("conv_norm_b")),
        w_out=w["w_out_bf"],
        mix_post=w["mix_norm_post"][:, None, :],
    )


def _token_tile(n):
    for tm in (1024, 512, 256, 128, 64, 32, 16, 8):
        if n % tm == 0:
            return tm
    raise ValueError(n)


def _ffn_tile(f):
    for tf in (256, 128):
        if f % tf == 0:
            return tf
    return f


def _group_layer(x, lw, s0, buf, tail_mat, attend, kv_t, ffn_bf):
    bsz, t, d = x.shape
    n = bsz * t
    tm = _token_tile(n)
    tf = _ffn_tile(lw["ffn1_f32"][0].shape[2])
    made_bf = {}

    def ffn(which, xin, mix=None):
        if ffn_bf is not None:
            return _ffn(xin, *lw[which + "_gains"], *ffn_bf[which], lw["layer"], tm=tm, tf=tf, mix=mix)
        out, *made_bf[which] = _ffn_stream(xin, *lw[which + "_gains"], *lw[which + "_f32"], lw["layer"], tm=tm, tf=tf,
                                           mix=mix)
        return out

    x2 = ffn("ffn1", x.reshape(n, d))
    gq, gk, gv, gg, la, sq, sk, sv, u = _mixin(x2, *lw["mixin"], tm=min(tm, 512, t) if kv_t else min(tm, 512),
                                               seq_len=t, kv_t=kv_t)
    seq = lambda a: a.reshape(bsz, t, a.shape[-1])
    o_gla, st = _gla(seq(gq), seq(gk), seq(gv), seq(la), seq(gg), _state_to_t(s0), lw["gla_norm"])
    o_sb = attend(seq(sq), sk, sv) if kv_t else attend(seq(sq), seq(sk), seq(sv))
    y, buf_new = _conv(seq(u), buf, *lw["conv"])
    mix = (o_gla.reshape(n, GLA_V), o_sb.reshape(n, SB_W), y.reshape(n, CONV_CH), lw["w_out"], lw["mix_post"])
    x4 = ffn("ffn2", x2, mix)
    if kv_t:
        heads = lambda a: a.reshape(bsz, SB_HEADS, SB_DIM, t)
    else:
        heads = lambda a: a.reshape(bsz, t, SB_HEADS, SB_DIM)
    return x4.reshape(bsz, t, d), heads(sk), heads(sv), _state_from_t(st), buf_new, made_bf


def kernel(x_prompt, x_sample, cache_sb_k, cache_sb_v, page_table, state_gla, state_conv, ffn1_norm_pre, ffn1_norm_post, ffn1_w_gate, ffn1_w_up, ffn1_w_down, mix_norm_pre, mix_norm_post, w_in, gla_w_a2, gla_b_a, gla_norm, sb_bias, conv_w, conv_b, conv_norm_g, conv_norm_b, w_out, ffn2_norm_pre, ffn2_norm_post, ffn2_w_gate, ffn2_w_up, ffn2_w_down):
    weights = dict(ffn1_norm_pre=ffn1_norm_pre, ffn1_norm_post=ffn1_norm_post, ffn1_w_gate=ffn1_w_gate,
                   ffn1_w_up=ffn1_w_up, ffn1_w_down=ffn1_w_down, mix_norm_pre=mix_norm_pre,
                   mix_norm_post=mix_norm_post, w_in=w_in, gla_w_a2=gla_w_a2, gla_b_a=gla_b_a, gla_norm=gla_norm,
                   sb_bias=sb_bias, conv_w=conv_w, conv_b=conv_b, conv_norm_g=conv_norm_g,
                   conv_norm_b=conv_norm_b, w_out=w_out, ffn2_norm_pre=ffn2_norm_pre,
                   ffn2_norm_post=ffn2_norm_post, ffn2_w_gate=ffn2_w_gate, ffn2_w_up=ffn2_w_up,
                   ffn2_w_down=ffn2_w_down)
    weights["w_out_bf"] = w_out.astype(BF16)
    depth = w_in.shape[0]
    bsz = x_prompt.shape[0]
    n_pool, page = cache_sb_k.shape[1], cache_sb_k.shape[2]
    tail_mat = _tail_matrix()
    ck = jnp.transpose(cache_sb_k, (0, 1, 3, 4, 2)).reshape(depth, n_pool, SB_W, page)
    cv = jnp.transpose(cache_sb_v, (0, 1, 3, 4, 2)).reshape(depth, n_pool, SB_W, page)
    xp, xs = x_prompt, x_sample
    outs = [[] for _ in range(8)]
    for l in range(depth):
        lw = _layer_weights(l, weights)
        s0 = jnp.zeros((bsz, GLA_HEADS, GLA_DK, GLA_DV), F32)
        buf0 = jnp.zeros((bsz, CONV_WIDTH - 1, CONV_CH), F32)
        decode_attend = lambda q, k, v, l=l: _sb_decode(q, k, v, lw["sb_bias"], tail_mat, ck, cv, page_table, l)
        xs, k2, v2, s2, c2, ffn_bf = _group_layer(xs, lw, state_gla[l], state_conv[l], tail_mat, decode_attend, False,
                                                  None)
        prompt_attend = lambda q, k, v: _sb_prompt(q, k, v, lw["sb_bias_rows"], tail_mat)
        xp, k1, v1, s1, c1, _ = _group_layer(xp, lw, s0, buf0, tail_mat, prompt_attend, True, ffn_bf)
        for lst, val in zip(outs, (k1, v1, s1, c1, k2, v2, s2, c2)):
            lst.append(val)
    stacked = [jnp.stack(o) for o in outs]
    for idx in (0, 1):
        stacked[idx] = jnp.transpose(stacked[idx], (0, 1, 4, 2, 3))
    return (xp, xs) + tuple(stacked)
```

```python
import functools
import math

import jax
import jax.numpy as jnp
from jax import lax
from jax.experimental import pallas as pl
from jax.experimental.pallas import tpu as pltpu

F32 = jnp.float32
BF16 = jnp.bfloat16

EPS = 1e-6
GLA_HEADS = 4
GLA_DK = 64
GLA_DV = 128
GLA_RANK = 16
GLA_TAU = 16.0
GLA_CHUNK = 128
SB_HEADS = 4
SB_DIM = 64
CONV_CH = 256
CONV_WIDTH = 31
GLA_QK = GLA_HEADS * GLA_DK
GLA_V = GLA_HEADS * GLA_DV
SB_W = SB_HEADS * SB_DIM

LANES = 128
SUBLANES = 8
SB_TILE = 128
SB_TILES_PER_TRIP = 4
HALO = 32
CONV_TILE = 256
CONV_SUBTILE = 64
GLA_SAFE_DECAY = 60.0
GLA_SEQS_PER_STEP = 4
PAGES_PER_GROUP = 16
DECODE_SLOTS = 4
VMEM_LIMIT = 48 * 1024 * 1024
FFN_VMEM_LIMIT = 56 * 1024 * 1024


def _cparams(sem):
    return pltpu.CompilerParams(dimension_semantics=sem, vmem_limit_bytes=VMEM_LIMIT)


def _rms(x, g):
    return x * lax.rsqrt(jnp.mean(x * x, axis=-1, keepdims=True) + EPS) * g


def _sigmoid(x):
    return 1.0 / (1.0 + jnp.exp(-x))


def _log_sigmoid(x):
    return jnp.minimum(x, 0.0) - jnp.log(1.0 + jnp.exp(-jnp.abs(x)))


def _split2(x):
    hi = x.astype(BF16)
    lo = (x - hi.astype(F32)).astype(BF16)
    return hi, lo


def _split3(x):
    hi = x.astype(BF16)
    r = x - hi.astype(F32)
    mid = r.astype(BF16)
    lo = (r - mid.astype(F32)).astype(BF16)
    return hi, mid, lo


def _dot(a, b):
    return jnp.dot(a, b, preferred_element_type=F32)


def _dot_nt(a, b):
    return lax.dot_general(a, b, (((1,), (1,)), ((), ())), preferred_element_type=F32)


def _dot_tn(a, b):
    return lax.dot_general(a, b, (((0,), (0,)), ((), ())), preferred_element_type=F32)


def _ffn_body(*refs, TF, MIX):
    if MIX:
        x_in_ref, og_ref, osb_ref, y_ref, wout_ref, gmix_ref = refs[:6]
        gpre_ref, gpost_ref, wg_ref, wu_ref, wd_ref, o_ref, h_scr, acc_scr = refs[6:]
        m = _dot(og_ref[...].astype(BF16), wout_ref[0:GLA_V, :])
        m = m + _dot(osb_ref[...].astype(BF16), wout_ref[GLA_V:GLA_V + SB_W, :])
        m = m + _dot(y_ref[...].astype(BF16), wout_ref[GLA_V + SB_W:, :])
        o_ref[...] = x_in_ref[...] + _rms(m, gmix_ref[...])
        x_ref = o_ref
    else:
        x_ref, gpre_ref, gpost_ref, wg_ref, wu_ref, wd_ref, o_ref, h_scr, acc_scr = refs
    h_scr[...] = _rms(x_ref[...], gpre_ref[...]).astype(BF16)
    nf = wg_ref.shape[1] // TF

    def chunk(f):
        cols = pl.ds(pl.multiple_of(f * TF, TF), TF)
        h = h_scr[...]
        g = _dot(h, wg_ref[:, cols])
        u = _dot(h, wu_ref[:, cols])
        a = (g * _sigmoid(g)) * u
        return _dot(a.astype(BF16), wd_ref[cols, :])

    acc_scr[...] = chunk(0)

    def body(f, carry):
        acc_scr[...] += chunk(f)
        return carry

    lax.fori_loop(1, nf, body, 0, unroll=2 if (nf - 1) % 2 == 0 else 1)
    o_ref[...] = x_ref[...] + 0.5 * _rms(acc_scr[...], gpost_ref[...])


def _ffn_stream_body(*refs, MIX):
    if MIX:
        x_in_ref, og_ref, osb_ref, y_ref, wout_ref, gmix_ref = refs[:6]
        refs = refs[6:]
    else:
        x_in_ref, refs = refs[0], refs[1:]
    gpre_ref, gpost_ref, wg_ref, wu_ref, wd_ref, o_ref, wg_out, wu_out, wd_out, h_scr, acc_scr = refs
    f = pl.program_id(1)

    @pl.when(f == 0)
    def _():
        x = x_in_ref[...]
        if MIX:
            m = _dot(og_ref[...].astype(BF16), wout_ref[0:GLA_V, :])
            m = m + _dot(osb_ref[...].astype(BF16), wout_ref[GLA_V:GLA_V + SB_W, :])
            m = m + _dot(y_ref[...].astype(BF16), wout_ref[GLA_V + SB_W:, :])
            x = x + _rms(m, gmix_ref[...])
        o_ref[...] = x
        h_scr[...] = _rms(x, gpre_ref[...]).astype(BF16)
        acc_scr[...] = jnp.zeros_like(acc_scr)

    wg, wu, wd = wg_ref[...].astype(BF16), wu_ref[...].astype(BF16), wd_ref[...].astype(BF16)
    wg_out[...] = wg
    wu_out[...] = wu
    wd_out[...] = wd
    h = h_scr[...]
    g = _dot(h, wg)
    u = _dot(h, wu)
    acc_scr[...] += _dot(((g * _sigmoid(g)) * u).astype(BF16), wd)

    @pl.when(f == pl.num_programs(1) - 1)
    def _():
        o_ref[...] = o_ref[...] + 0.5 * _rms(acc_scr[...], gpost_ref[...])


def _ffn_stream(x, g_pre, g_post, w_gate, w_up, w_down, layer, *, tm, tf, mix=None):
    n, d = x.shape
    hidden = w_gate.shape[2]
    const = lambda a: pl.BlockSpec((None,) + a.shape[1:], lambda i, f: (layer, 0, 0))
    gain = pl.BlockSpec((None, 1, d), lambda i, f: (layer, 0, 0))
    tok = lambda w: pl.BlockSpec((tm, w), lambda i, f: (i, 0))
    in_specs, args = [tok(d)], [x]
    if mix is not None:
        og, osb, y, w_out, g_mix = mix
        in_specs += [tok(GLA_V), tok(SB_W), tok(CONV_CH), const(w_out), gain]
        args += [og, osb, y, w_out, g_mix]
    in_specs += [gain, gain,
                 pl.BlockSpec((None, d, tf), lambda i, f: (layer, 0, f)),
                 pl.BlockSpec((None, d, tf), lambda i, f: (layer, 0, f)),
                 pl.BlockSpec((None, tf, d), lambda i, f: (layer, f, 0))]
    args += [g_pre, g_post, w_gate, w_up, w_down]
    return pl.pallas_call(
        functools.partial(_ffn_stream_body, MIX=mix is not None),
        grid=(n // tm, hidden // tf),
        in_specs=in_specs,
        out_specs=[tok(d), pl.BlockSpec((d, tf), lambda i, f: (0, f)), pl.BlockSpec((d, tf), lambda i, f: (0, f)),
                   pl.BlockSpec((tf, d), lambda i, f: (f, 0))],
        out_shape=[jax.ShapeDtypeStruct((n, d), F32), jax.ShapeDtypeStruct((d, hidden), BF16),
                   jax.ShapeDtypeStruct((d, hidden), BF16), jax.ShapeDtypeStruct((hidden, d), BF16)],
        scratch_shapes=[pltpu.VMEM((tm, d), BF16), pltpu.VMEM((tm, d), F32)],
        compiler_params=_cparams(("arbitrary", "arbitrary")),
        name="half_ffn_stream" if mix is None else "mixer_out_ffn_stream",
    )(*args)


def _ffn(x, g_pre, g_post, w_gate, w_up, w_down, layer, *, tm, tf, mix=None):
    n, d = x.shape
    resident = lambda a: pl.BlockSpec(a.shape, lambda i: (0, 0), pipeline_mode=pl.Buffered(1))
    gain = pl.BlockSpec((None, 1, d), lambda i: (layer, 0, 0))
    tok = lambda w: pl.BlockSpec((tm, w), lambda i: (i, 0))
    in_specs, args = [tok(d)], [x]
    if mix is not None:
        og, osb, y, w_out, g_mix = mix
        in_specs += [tok(GLA_V), tok(SB_W), tok(CONV_CH),
                     pl.BlockSpec((None,) + w_out.shape[1:], lambda i: (layer, 0, 0), pipeline_mode=pl.Buffered(1)),
                     gain]
        args += [og, osb, y, w_out, g_mix]
    in_specs += [gain, gain, resident(w_gate), resident(w_up), resident(w_down)]
    args += [g_pre, g_post, w_gate, w_up, w_down]
    return pl.pallas_call(
        functools.partial(_ffn_body, TF=tf, MIX=mix is not None),
        grid=(n // tm,),
        in_specs=in_specs,
        out_specs=tok(d),
        out_shape=jax.ShapeDtypeStruct((n, d), F32),
        scratch_shapes=[pltpu.VMEM((tm, d), BF16), pltpu.VMEM((tm, d), F32)],
        compiler_params=pltpu.CompilerParams(dimension_semantics=("parallel",),
                                             vmem_limit_bytes=FFN_VMEM_LIMIT if mix is not None else VMEM_LIMIT),
        name="half_ffn" if mix is None else "mixer_out_ffn",
    )(*args)


_SEG = {}
_off = 0
for _name, _w in (("gq", GLA_QK), ("gk", GLA_QK), ("gv", GLA_V), ("gg", GLA_V), ("sq", SB_W), ("sk", SB_W),
                  ("sv", SB_W), ("ca", CONV_CH), ("cg", CONV_CH)):
    _SEG[_name] = (_off, _off + _w)
    _off += _w
MAIN_COLS = _off


def _mixin_body(x_ref, g_ref, w_ref, wlr_ref, wa2_ref, ba_ref, *rest, KV_T, KV_PREV, LAYER):
    if KV_T:
        wkvt_ref, rest = rest[0], rest[1:]
    if KV_PREV:
        rest = rest[2:]
    gq_ref, gk_ref, gv_ref, gg_ref, la_ref, sq_ref, sk_ref, sv_ref, u_ref = rest
    h = _rms(x_ref[...], g_ref[...]).astype(BF16)

    def seg(name):
        lo, hi = _SEG[name]
        return _dot(h, w_ref[:, lo:hi])

    gq_ref[...] = seg("gq") * (GLA_DK ** -0.5)
    gk_ref[...] = seg("gk")
    gv_ref[...] = seg("gv")
    gg_ref[...] = seg("gg")
    sq_ref[...] = seg("sq")
    if KV_T and KV_PREV:
        sk_ref[0] = _dot_nt(wkvt_ref[0:SB_W, :], h)
        sv_ref[0] = _dot_nt(wkvt_ref[SB_W:2 * SB_W, :], h)
    elif KV_T:
        for other in range(sk_ref.shape[0]):
            if other != LAYER:
                sk_ref[other] = jnp.zeros(sk_ref.shape[1:], F32)
                sv_ref[other] = jnp.zeros(sv_ref.shape[1:], F32)
        sk_ref[LAYER, 0] = _dot_nt(wkvt_ref[0:SB_W, :], h)
        sv_ref[LAYER, 0] = _dot_nt(wkvt_ref[SB_W:2 * SB_W, :], h)
    else:
        sk_ref[...] = seg("sk")
        sv_ref[...] = seg("sv")
    u_ref[...] = seg("ca") * _sigmoid(seg("cg"))
    lr = _dot(h, wlr_ref[...])
    xa = _dot(lr.astype(BF16), wa2_ref[...]) + ba_ref[...]
    la_ref[...] = _log_sigmoid(xa) * (1.0 / GLA_TAU)


def _mixin(x, g, w_main, w_lr, w_a2, b_a, w_kvt, *, tm, seq_len, kv_t, layer=0, depth=1, kv_prev=None):
    aliases = {}
    n, d = x.shape
    widths = (GLA_QK, GLA_QK, GLA_V, GLA_V, GLA_QK, SB_W, SB_W, SB_W, CONV_CH)
    const = lambda shape: pl.BlockSpec(shape, lambda i: (0, 0))
    in_specs = [pl.BlockSpec((tm, d), lambda i: (i, 0)), const((1, d)), const(w_main.shape),
                const(w_lr.shape), const(w_a2.shape), const((1, GLA_QK))]
    out_specs = [pl.BlockSpec((tm, w), lambda i: (i, 0)) for w in widths]
    out_shape = [jax.ShapeDtypeStruct((n, w), F32) for w in widths]
    args = [x, g, w_main, w_lr, w_a2, b_a]
    if kv_t:
        assert seq_len % tm == 0
        per_seq = seq_len // tm
        in_specs.append(const(w_kvt.shape))
        args.append(w_kvt)
        for idx in (6, 7):
            if kv_prev is None:
                out_specs[idx] = pl.BlockSpec((depth, 1, SB_W, tm), lambda i: (0, i // per_seq, 0, i % per_seq))
            else:
                out_specs[idx] = pl.BlockSpec((None, 1, SB_W, tm), lambda i: (layer, i // per_seq, 0, i % per_seq))
            out_shape[idx] = jax.ShapeDtypeStruct((depth, n // seq_len, SB_W, seq_len), F32)
        if kv_prev is not None:
            aliases = {len(args): 6, len(args) + 1: 7}
            in_specs += [pl.BlockSpec(memory_space=pl.ANY)] * 2
            args += list(kv_prev)
    return pl.pallas_call(
        functools.partial(_mixin_body, KV_T=kv_t, KV_PREV=bool(aliases), LAYER=layer),
        grid=(n // tm,),
        in_specs=in_specs,
        out_specs=out_specs,
        out_shape=out_shape,
        input_output_aliases=aliases,
        compiler_params=_cparams(("parallel",)),
        name="mixer_in",
    )(*args)


def _gla_body(q_ref, k_ref, v_ref, la_ref, gg_ref, s0_ref, gn_ref, o_ref, sout_ref, st_scr, b_scr, oi_scr, qbs_scr,
              *, C, NB):
    c = pl.program_id(1)

    @pl.when(c == 0)
    def _():
        st_scr[...] = s0_ref[...]

    row = lax.broadcasted_iota(jnp.int32, (C, C), 0)
    col = lax.broadcasted_iota(jnp.int32, (C, C), 1)
    causal = row >= col
    tri = jnp.where(causal, 1.0, 0.0).astype(BF16)
    lane_head = lax.broadcasted_iota(jnp.int32, (C, GLA_QK), 1) // GLA_DK
    heads = [slice(h * GLA_DV, (h + 1) * GLA_DV) for h in range(GLA_HEADS)]

    def head_stack(x):
        return jnp.concatenate([jnp.where(lane_head == h, x, 0.0) for h in range(GLA_HEADS)], axis=0).astype(BF16)

    rows_t = lax.broadcasted_iota(jnp.int32, (GLA_HEADS * C, C), 0) % C
    causal_rows = rows_t >= lax.broadcasted_iota(jnp.int32, (GLA_HEADS * C, C), 1)

    decays = []
    for n in range(NB):
        q, k, v = q_ref[n], k_ref[n], v_ref[n]
        hi, mid, lo = _split3(la_ref[n])
        b = _dot(tri, jnp.concatenate([hi, mid, lo], axis=1))
        b = b[:, :GLA_QK] + b[:, GLA_QK:2 * GLA_QK] + b[:, 2 * GLA_QK:]
        b_scr[n] = b
        b_last = b[C - 1:C, :]
        decays.append(jnp.max(-b_last))
        st = st_scr[n]
        qb = q * jnp.exp(b)
        kd = k * jnp.exp(b_last - b)
        qbs = head_stack(qb)
        qbs_scr[n] = qbs
        oi = _dot_nt(qbs, st.astype(BF16))
        for h, sl in enumerate(heads):
            oi_scr[n, :, sl] = oi[h * C:(h + 1) * C]
        v_rows = jnp.concatenate([v[:, sl] for sl in heads], axis=0).astype(BF16)
        st_scr[n] = st * jnp.exp(b_last) + _dot_tn(v_rows, head_stack(kd))
    total_decay = functools.reduce(jnp.maximum, decays)

    @pl.when(total_decay <= GLA_SAFE_DECAY)
    def _():
        for n in range(NB):
            k, v, b = k_ref[n], v_ref[n], b_scr[n]
            kb = (k * jnp.exp(-b)).astype(BF16)
            attn = jnp.where(causal_rows, _dot_nt(qbs_scr[n], kb), 0.0).astype(BF16)
            for h, sl in enumerate(heads):
                oi_scr[n, :, sl] += _dot(attn[h * C:(h + 1) * C], v[:, sl].astype(BF16))

    @pl.when(total_decay > GLA_SAFE_DECAY)
    def _():
        rhead = lax.broadcasted_iota(jnp.int32, (GLA_QK, GLA_V), 0) // GLA_DK
        chead = lax.broadcasted_iota(jnp.int32, (GLA_QK, GLA_V), 1) // GLA_DV
        spread = jnp.where(rhead == chead, 1.0, 0.0).astype(BF16)
        trow = lax.broadcasted_iota(jnp.int32, (C, GLA_QK), 0)
        for n in range(NB):
            q, b = q_ref[n], b_scr[n]

            def body(s, acc):
                bs = b_scr[n, pl.ds(s, 1), :]
                ks = k_ref[n, pl.ds(s, 1), :]
                vs = v_ref[n, pl.ds(s, 1), :]
                p = q * ks * jnp.exp(jnp.minimum(b - bs, 0.0))
                p = jnp.where(trow >= s, p, 0.0)
                p_hi, p_mid, p_lo = _split3(p)
                r = _dot(p_hi, spread) + _dot(p_mid, spread) + _dot(p_lo, spread)
                return acc + r * vs

            oi_scr[n] += lax.fori_loop(0, C, body, jnp.zeros((C, GLA_V), F32))

    for n in range(NB):
        o, gg = oi_scr[n], gg_ref[n]
        for sl in heads:
            o_ref[n, :, sl] = _rms(o[:, sl], gn_ref[:, sl]) * (gg[:, sl] * _sigmoid(gg[:, sl]))

    @pl.when(c == pl.num_programs(1) - 1)
    def _():
        sout_ref[...] = st_scr[...]


def _gla(q, k, v, la, gg, s0_t, gnorm):
    bsz, t_real, _ = q.shape
    if t_real < 16:
        padt = lambda a: jnp.pad(a, ((0, 0), (0, 16 - t_real), (0, 0)))
        q, k, v, la, gg = (padt(a) for a in (q, k, v, la, gg))
    t = q.shape[1]
    c = min(GLA_CHUNK, t)
    nb = GLA_SEQS_PER_STEP if bsz % GLA_SEQS_PER_STEP == 0 else 1
    assert t % c == 0
    tok = lambda w: pl.BlockSpec((nb, c, w), lambda b, i: (b, i, 0))
    st = pl.BlockSpec((nb, GLA_DV, GLA_QK), lambda b, i: (b, 0, 0))
    o, st_new = pl.pallas_call(
        functools.partial(_gla_body, C=c, NB=nb),
        grid=(bsz // nb, t // c),
        in_specs=[tok(GLA_QK), tok(GLA_QK), tok(GLA_V), tok(GLA_QK), tok(GLA_V), st,
                  pl.BlockSpec((1, GLA_V), lambda b, i: (0, 0))],
        out_specs=[tok(GLA_V), st],
        out_shape=[jax.ShapeDtypeStruct((bsz, t, GLA_V), F32), jax.ShapeDtypeStruct((bsz, GLA_DV, GLA_QK), F32)],
        scratch_shapes=[pltpu.VMEM((nb, GLA_DV, GLA_QK), F32), pltpu.VMEM((nb, c, GLA_QK), F32),
                        pltpu.VMEM((nb, c, GLA_V), F32), pltpu.VMEM((nb, GLA_HEADS * c, GLA_QK), BF16)],
        compiler_params=_cparams(("parallel", "arbitrary")),
        name="gla",
    )(q, k, v, la, gg, s0_t, gnorm)
    return o[:, :t_real], st_new


def _state_to_t(s):
    b = s.shape[0]
    return jnp.transpose(s, (0, 3, 1, 2)).reshape(b, GLA_DV, GLA_QK)


def _state_from_t(st):
    b = st.shape[0]
    return jnp.transpose(st.reshape(b, GLA_DV, GLA_HEADS, GLA_DK), (0, 2, 3, 1))


def _sb_weights(z, tail_mat_ref, carry, mask):
    ls = _log_sigmoid(z)
    l1m = ls - z
    if mask is not None:
        l1m = jnp.where(mask, l1m, 0.0)
    hi, lo = _split2(l1m)
    cs = _dot(jnp.concatenate([hi, lo], axis=1), tail_mat_ref[...])
    w = jnp.exp(ls + cs[:, :LANES] + carry)
    if mask is not None:
        w = jnp.where(mask, w, 0.0)
    return w, carry + cs[:, LANES:]


def _tail_matrix():
    src = jnp.arange(2 * LANES)[:, None] % LANES
    dst = jnp.arange(2 * LANES)[None, :]
    return jnp.where((dst >= LANES) | (src > dst), 1.0, 0.0).astype(BF16)


def _sb_prompt_body(bias_ref, q_ref, k_ref, v_ref, tm_ref, o_ref, kbf_scr, vbd_scr, qbd_scr, carry_scr, acc_scr,
                    *, T):
    i = pl.program_id(1)
    nkb = T // SB_TILE
    R = SB_HEADS * SB_TILE
    lane_head = lax.broadcasted_iota(jnp.int32, (SB_TILE, SB_W), 1) // SB_DIM

    @pl.when(i == 0)
    def _():
        kbf_scr[...] = k_ref[0].astype(BF16)
        for jb in range(nkb):
            vb = v_ref[0, :, jb * SB_TILE:(jb + 1) * SB_TILE].T
            for h in range(SB_HEADS):
                vbd_scr[jb, h * SB_TILE:(h + 1) * SB_TILE, :] = jnp.where(lane_head == h, vb, 0.0).astype(BF16)

    q = q_ref[0] * (SB_DIM ** -0.5)
    for h in range(SB_HEADS):
        qbd_scr[h * SB_TILE:(h + 1) * SB_TILE, :] = jnp.where(lane_head == h, q, 0.0).astype(BF16)
    carry_scr[...] = jnp.zeros_like(carry_scr)
    acc_scr[...] = jnp.zeros_like(acc_scr)

    def tile(j, mask):
        kblk = kbf_scr[:, pl.ds(pl.multiple_of(j * SB_TILE, SB_TILE), SB_TILE)]
        z = _dot(qbd_scr[...], kblk) + bias_ref[...]
        w, carry_scr[...] = _sb_weights(z, tm_ref, carry_scr[...], mask)
        wb = w.astype(BF16)
        wcat = jnp.concatenate([wb[h * SB_TILE:(h + 1) * SB_TILE] for h in range(SB_HEADS)], axis=1)
        acc_scr[...] += _dot(wcat, vbd_scr[j])

    trow = lax.broadcasted_iota(jnp.int32, (R, SB_TILE), 0) % SB_TILE
    scol = lax.broadcasted_iota(jnp.int32, (R, SB_TILE), 1)
    rem = i % SB_TILES_PER_TRIP
    for r in range(SB_TILES_PER_TRIP):
        @pl.when(rem == r)
        def _(r=r):
            tile(i, scol < trow)
            for u in range(r):
                tile(i - 1 - u, None)

    def group(jj, _):
        for u in range(SB_TILES_PER_TRIP):
            tile(i - rem - 1 - u - SB_TILES_PER_TRIP * jj, None)
        return 0

    lax.fori_loop(0, i // SB_TILES_PER_TRIP, group, 0)

    o_ref[0] = acc_scr[...]


def _sb_prompt(q, k_t, v_t, bias_rows, tail_mat, layer):
    bsz, t, w = q.shape
    nq = t // SB_TILE
    full = pl.BlockSpec((None, 1, w, t), lambda b, i: (layer, b, 0, 0))
    qt = pl.BlockSpec((1, SB_TILE, w), lambda b, i: (b, i, 0))
    const = lambda a: pl.BlockSpec(a.shape, lambda b, i: (0, 0))
    return pl.pallas_call(
        functools.partial(_sb_prompt_body, T=t),
        grid=(bsz, nq),
        in_specs=[const(bias_rows), qt, full, full, const(tail_mat)],
        out_specs=qt,
        out_shape=jax.ShapeDtypeStruct((bsz, t, w), F32),
        scratch_shapes=[pltpu.VMEM((w, t), BF16), pltpu.VMEM((nq, SB_HEADS * SB_TILE, w), BF16),
                        pltpu.VMEM((SB_HEADS * SB_TILE, w), BF16), pltpu.VMEM((SB_HEADS * SB_TILE, LANES), F32),
                        pltpu.VMEM((SB_TILE, w), F32)],
        compiler_params=_cparams(("parallel", "arbitrary")),
        name="sb_prompt",
    )(bias_rows, q, k_t, v_t, tail_mat)


def _sb_decode_body(pt_ref, bias_ref, q_ref, kn_ref, vn_ref, tm_ref, kc_hbm, vc_hbm, o_ref,
                    kbuf, vbuf, sem, knew_scr, vnew_scr, carry_scr, acc_scr, *, TQ, NP, G, NB, NSLOT, LAYER):
    b = pl.program_id(0)
    nb = NB
    ng = NP // G
    R = SB_HEADS * TQ

    def copies(bb, grp, slot):
        out = []
        for p in range(G):
            page = pt_ref[bb, grp * G + p]
            keys = pl.ds(p * SB_TILE, SB_TILE)
            out.append(pltpu.make_async_copy(kc_hbm.at[LAYER, page], kbuf.at[slot, :, keys], sem.at[0, slot]))
            out.append(pltpu.make_async_copy(vc_hbm.at[LAYER, page], vbuf.at[slot, :, keys], sem.at[1, slot]))
        return out

    def start_group(step):
        for cp in copies(step // ng, ng - 1 - step % ng, step % NSLOT):
            cp.start()

    @pl.when(b == 0)
    def _():
        for s in range(min(NSLOT - 1, nb * ng)):
            start_group(s)

    lane_head = lax.broadcasted_iota(jnp.int32, (TQ, SB_W), 1) // SB_DIM
    q = q_ref[0] * (SB_DIM ** -0.5)
    qbd = jnp.concatenate([jnp.where(lane_head == h, q, 0.0) for h in range(SB_HEADS)], axis=0).astype(BF16)
    bias = jnp.concatenate([jnp.full((TQ, LANES), bias_ref[h], F32) for h in range(SB_HEADS)], axis=0)

    knew_scr[...] = jnp.zeros_like(knew_scr)
    vnew_scr[...] = jnp.zeros_like(vnew_scr)
    knew_scr[0:TQ, :] = kn_ref[0]
    vnew_scr[0:TQ, :] = vn_ref[0]
    trow = lax.broadcasted_iota(jnp.int32, (R, LANES), 0) % TQ
    scol = lax.broadcasted_iota(jnp.int32, (R, LANES), 1)
    z = _dot_nt(qbd, knew_scr[...].astype(BF16)) + bias
    w, carry = _sb_weights(z, tm_ref, jnp.zeros((R, LANES), F32), scol < trow)
    carry_scr[...] = carry
    acc_scr[...] = _dot(w.astype(BF16), vnew_scr[...].astype(BF16))

    def group(r, _):
        step = b * ng + r
        slot = step % NSLOT
        for cp in copies(b, ng - 1 - r, slot):
            cp.wait()

        @pl.when(step + NSLOT - 1 < nb * ng)
        def _():
            start_group(step + NSLOT - 1)

        kg = kbuf[slot].astype(BF16)
        vg = vbuf[slot].astype(BF16)
        zg = _dot(qbd, kg)
        zrows = jnp.concatenate([zg[:, p * LANES:(p + 1) * LANES] + bias for p in range(G)], axis=0)
        ls = _log_sigmoid(zrows)
        hi, lo = _split2(ls - zrows)
        cs = _dot(jnp.concatenate([hi, lo], axis=1), tm_ref[...])
        carry = carry_scr[...]
        ws = [None] * G
        for p in reversed(range(G)):
            rs = slice(p * R, (p + 1) * R)
            ws[p] = jnp.exp(ls[rs] + cs[rs, :LANES] + carry).astype(BF16)
            carry = carry + cs[rs, LANES:]
        carry_scr[...] = carry
        acc_scr[...] += _dot_nt(jnp.concatenate(ws, axis=1), vg)
        return 0

    lax.fori_loop(0, ng, group, 0)
    acc = acc_scr[...]
    out = jnp.zeros((TQ, SB_W), F32)
    for h in range(SB_HEADS):
        out = out + jnp.where(lane_head == h, acc[h * TQ:(h + 1) * TQ, :], 0.0)
    o_ref[0] = out


def _sb_decode(q, k_new, v_new, bias, tail_mat, cache_k, cache_v, page_table, layer):
    db, tq, w = q.shape
    npages = page_table.shape[1]
    g = math.gcd(npages, PAGES_PER_GROUP)
    assert cache_k.shape[2:] == (w, SB_TILE)
    tok = pl.BlockSpec((1, tq, w), lambda b, pt: (b, 0, 0))
    grid_spec = pltpu.PrefetchScalarGridSpec(
        num_scalar_prefetch=1,
        grid=(db,),
        in_specs=[pl.BlockSpec(memory_space=pltpu.SMEM), tok, tok, tok,
                  pl.BlockSpec(tail_mat.shape, lambda b, pt: (0, 0)),
                  pl.BlockSpec(memory_space=pl.ANY), pl.BlockSpec(memory_space=pl.ANY)],
        out_specs=tok,
        scratch_shapes=[pltpu.VMEM((DECODE_SLOTS, w, g * SB_TILE), F32), pltpu.VMEM((DECODE_SLOTS, w, g * SB_TILE), F32),
                        pltpu.SemaphoreType.DMA((2, DECODE_SLOTS)),
                        pltpu.VMEM((SB_TILE, w), F32), pltpu.VMEM((SB_TILE, w), F32),
                        pltpu.VMEM((SB_HEADS * tq, LANES), F32), pltpu.VMEM((SB_HEADS * tq, w), F32)],
    )
    return pl.pallas_call(
        functools.partial(_sb_decode_body, TQ=tq, NP=npages, G=g, NB=db, NSLOT=DECODE_SLOTS,
                          LAYER=layer),
        grid_spec=grid_spec,
        out_shape=jax.ShapeDtypeStruct((db, tq, w), F32),
        compiler_params=_cparams(("arbitrary",)),
        name="sb_decode",
    )(page_table, bias, q, k_new, v_new, tail_mat, cache_k, cache_v)


def _conv_body(u_ref, prev_ref, buf_ref, w_ref, cb_ref, lg_ref, lb_ref, y_ref, bufo_ref, win_scr, *, RT, NT):
    i = pl.program_id(1)
    hist = CONV_WIDTH - 1
    first = HALO - hist

    @pl.when(i == 0)
    def _():
        win_scr[first:HALO, :] = buf_ref[0]

    if NT > 1:
        @pl.when(i > 0)
        def _():
            win_scr[0:HALO, :] = prev_ref[0]

    win_scr[HALO:HALO + RT, :] = u_ref[0]
    st = min(RT, CONV_SUBTILE)
    n = HALO + st
    for base in range(0, RT, st):
        win = win_scr[base:base + n, :]
        acc = jnp.zeros((st, CONV_CH), F32)
        for r in range(SUBLANES):
            offs = [o for o in range(first, first + CONV_WIDTH) if o % SUBLANES == r]
            shifted = win if r == 0 else pltpu.roll(win, n - r, 0)
            for o in offs:
                acc = acc + shifted[o - r:o - r + st, :] * w_ref[o - first:o - first + 1, :]
        y = acc + cb_ref[...]
        mu = jnp.mean(y, axis=-1, keepdims=True)
        yc = y - mu
        yn = yc * lax.rsqrt(jnp.mean(yc * yc, axis=-1, keepdims=True) + EPS) * lg_ref[...] + lb_ref[...]
        y_ref[0, base:base + st, :] = yn * _sigmoid(yn)

    @pl.when(i == pl.num_programs(1) - 1)
    def _():
        bufo_ref[0] = win_scr[first + RT:HALO + RT, :]


def _conv(u, buf, w, cb, lg, lb):
    bsz, t, ch = u.shape
    hist = CONV_WIDTH - 1
    rt = min(t, CONV_TILE)
    nt = t // rt
    assert t % rt == 0 and (nt == 1 or rt % HALO == 0)
    per = rt // HALO if nt > 1 else 1
    tile = pl.BlockSpec((1, rt, ch), lambda b, i: (b, i, 0))
    prev_rows = HALO if nt > 1 else rt
    prev = pl.BlockSpec((1, prev_rows, ch), lambda b, i: (b, jnp.maximum(i * per - 1, 0), 0))
    state = pl.BlockSpec((1, hist, ch), lambda b, i: (b, 0, 0))
    vec = lambda r: pl.BlockSpec((r, ch), lambda b, i: (0, 0))
    return pl.pallas_call(
        functools.partial(_conv_body, RT=rt, NT=nt),
        grid=(bsz, nt),
        in_specs=[tile, prev, state, vec(CONV_WIDTH), vec(1), vec(1), vec(1)],
        out_specs=[tile, state],
        out_shape=[jax.ShapeDtypeStruct((bsz, t, ch), F32), jax.ShapeDtypeStruct((bsz, hist, ch), F32)],
        scratch_shapes=[pltpu.VMEM((HALO + rt, ch), F32)],
        compiler_params=_cparams(("parallel", "arbitrary")),
        name="conv_branch",
    )(u, u, buf, w, cb, lg, lb)


def _layer_weights(l, w):
    d = w["w_in"].shape[1]
    w_in = w["w_in"][l]
    sizes = (GLA_QK, GLA_QK, GLA_V, GLA_V, GLA_RANK, SB_W, SB_W, SB_W, 2 * CONV_CH)
    offs = [0]
    for s in sizes:
        offs.append(offs[-1] + s)
    main = jnp.concatenate([w_in[:, :offs[4]], w_in[:, offs[5]:]], axis=1).astype(BF16)
    w_lr = jnp.zeros((d, LANES), F32).at[:, :GLA_RANK].set(w_in[:, offs[4]:offs[5]]).astype(BF16)
    w_a2 = jnp.zeros((LANES, GLA_QK), F32).at[:GLA_RANK, :].set(w["gla_w_a2"][l]).astype(BF16)
    row = lambda name: w[name][l][None, :]
    return dict(
        layer=l,
        ffn1_gains=(w["ffn1_norm_pre"][:, None, :], w["ffn1_norm_post"][:, None, :]),
        ffn2_gains=(w["ffn2_norm_pre"][:, None, :], w["ffn2_norm_post"][:, None, :]),
        ffn1_f32=(w["ffn1_w_gate"], w["ffn1_w_up"], w["ffn1_w_down"]),
        ffn2_f32=(w["ffn2_w_gate"], w["ffn2_w_up"], w["ffn2_w_down"]),
        mixin=(row("mix_norm_pre"), main, w_lr, w_a2, row("gla_b_a"), w_in[:, offs[6]:offs[8]].T.astype(BF16)),
        gla_norm=row("gla_norm"),
        sb_bias=w["sb_bias"][l],
        sb_bias_rows=jnp.broadcast_to(jnp.repeat(w["sb_bias"][l], SB_TILE)[:, None], (SB_HEADS * SB_TILE, LANES)),
        conv=(w["conv_w"][l], row("conv_b"), row("conv_norm_g"), row("conv_norm_b")),
        w_out=w["w_out_bf"],
        mix_post=w["mix_norm_post"][:, None, :],
    )


def _token_tile(n):
    for tm in (1024, 512, 256, 128, 64, 32, 16, 8):
        if n % tm == 0:
            return tm
    raise ValueError(n)


def _ffn_tile(f):
    for tf in (256, 128):
        if f % tf == 0:
            return tf
    return f


def _group_layer(x, lw, s0, buf, tail_mat, attend, kv_t, ffn_bf):
    bsz, t, d = x.shape
    n = bsz * t
    tm = _token_tile(n)
    tf = _ffn_tile(lw["ffn1_f32"][0].shape[2])
    made_bf = {}

    def ffn(which, xin, mix=None):
        if ffn_bf is not None:
            return _ffn(xin, *lw[which + "_gains"], *ffn_bf[which], lw["layer"], tm=tm, tf=tf, mix=mix)
        out, *made_bf[which] = _ffn_stream(xin, *lw[which + "_gains"], *lw[which + "_f32"], lw["layer"], tm=tm, tf=tf,
                                           mix=mix)
        return out

    x2 = ffn("ffn1", x.reshape(n, d))
    gq, gk, gv, gg, la, sq, sk, sv, u = _mixin(x2, *lw["mixin"], tm=min(tm, 512, t) if kv_t else min(tm, 512),
                                               seq_len=t, kv_t=kv_t, layer=lw["layer"], depth=lw["depth"],
                                               kv_prev=lw["kv_prev"] if kv_t else None)
    seq = lambda a: a.reshape(bsz, t, a.shape[-1])
    o_gla, st = _gla(seq(gq), seq(gk), seq(gv), seq(la), seq(gg), _state_to_t(s0), lw["gla_norm"])
    o_sb = attend(seq(sq), sk, sv) if kv_t else attend(seq(sq), seq(sk), seq(sv))
    y, buf_new = _conv(seq(u), buf, *lw["conv"])
    mix = (o_gla.reshape(n, GLA_V), o_sb.reshape(n, SB_W), y.reshape(n, CONV_CH), lw["w_out"], lw["mix_post"])
    x4 = ffn("ffn2", x2, mix)
    if not kv_t:
        sk, sv = (a.reshape(bsz, t, SB_HEADS, SB_DIM) for a in (sk, sv))
    return x4.reshape(bsz, t, d), sk, sv, _state_from_t(st), buf_new, made_bf


def kernel(x_prompt, x_sample, cache_sb_k, cache_sb_v, page_table, state_gla, state_conv, ffn1_norm_pre, ffn1_norm_post, ffn1_w_gate, ffn1_w_up, ffn1_w_down, mix_norm_pre, mix_norm_post, w_in, gla_w_a2, gla_b_a, gla_norm, sb_bias, conv_w, conv_b, conv_norm_g, conv_norm_b, w_out, ffn2_norm_pre, ffn2_norm_post, ffn2_w_gate, ffn2_w_up, ffn2_w_down):
    weights = dict(ffn1_norm_pre=ffn1_norm_pre, ffn1_norm_post=ffn1_norm_post, ffn1_w_gate=ffn1_w_gate,
                   ffn1_w_up=ffn1_w_up, ffn1_w_down=ffn1_w_down, mix_norm_pre=mix_norm_pre,
                   mix_norm_post=mix_norm_post, w_in=w_in, gla_w_a2=gla_w_a2, gla_b_a=gla_b_a, gla_norm=gla_norm,
                   sb_bias=sb_bias, conv_w=conv_w, conv_b=conv_b, conv_norm_g=conv_norm_g,
                   conv_norm_b=conv_norm_b, w_out=w_out, ffn2_norm_pre=ffn2_norm_pre,
                   ffn2_norm_post=ffn2_norm_post, ffn2_w_gate=ffn2_w_gate, ffn2_w_up=ffn2_w_up,
                   ffn2_w_down=ffn2_w_down)
    weights["w_out_bf"] = w_out.astype(BF16)
    depth = w_in.shape[0]
    bsz = x_prompt.shape[0]
    n_pool, page = cache_sb_k.shape[1], cache_sb_k.shape[2]
    tail_mat = _tail_matrix()
    ck = jnp.transpose(cache_sb_k, (0, 1, 3, 4, 2)).reshape(depth, n_pool, SB_W, page)
    cv = jnp.transpose(cache_sb_v, (0, 1, 3, 4, 2)).reshape(depth, n_pool, SB_W, page)
    xp, xs = x_prompt, x_sample
    outs = [[] for _ in range(6)]
    prompt_kv = None
    for l in range(depth):
        lw = _layer_weights(l, weights)
        lw.update(depth=depth, kv_prev=prompt_kv)
        s0 = jnp.zeros((bsz, GLA_HEADS, GLA_DK, GLA_DV), F32)
        buf0 = jnp.zeros((bsz, CONV_WIDTH - 1, CONV_CH), F32)
        decode_attend = lambda q, k, v, l=l: _sb_decode(q, k, v, lw["sb_bias"], tail_mat, ck, cv, page_table, l)
        xs, k2, v2, s2, c2, ffn_bf = _group_layer(xs, lw, state_gla[l], state_conv[l], tail_mat, decode_attend, False,
                                                  None)
        prompt_attend = lambda q, k, v, l=l: _sb_prompt(q, k, v, lw["sb_bias_rows"], tail_mat, l)
        xp, k1, v1, s1, c1, _ = _group_layer(xp, lw, s0, buf0, tail_mat, prompt_attend, True, ffn_bf)
        prompt_kv = (k1, v1)
        for lst, val in zip(outs, (s1, c1, k2, v2, s2, c2)):
            lst.append(val)
    seq_len = x_prompt.shape[1]
    pk, pv = (jnp.transpose(a.reshape(depth, bsz, SB_HEADS, SB_DIM, seq_len), (0, 1, 4, 2, 3)) for a in prompt_kv)
    return (xp, xs, pk, pv) + tuple(jnp.stack(o) for o in outs)
```

```python
import functools
import math

import jax
import jax.numpy as jnp
from jax import lax
from jax.experimental import pallas as pl
from jax.experimental.pallas import tpu as pltpu

F32 = jnp.float32
BF16 = jnp.bfloat16

EPS = 1e-6
GLA_HEADS = 4
GLA_DK = 64
GLA_DV = 128
GLA_RANK = 16
GLA_TAU = 16.0
GLA_CHUNK = 128
SB_HEADS = 4
SB_DIM = 64
CONV_CH = 256
CONV_WIDTH = 31
GLA_QK = GLA_HEADS * GLA_DK
GLA_V = GLA_HEADS * GLA_DV
SB_W = SB_HEADS * SB_DIM

LANES = 128
SUBLANES = 8
SB_TILE = 128
SB_TILES_PER_TRIP = 8
HALO = 32
CONV_TILE = 512
CONV_SUBTILE = 64
GLA_SAFE_DECAY = 60.0
GLA_SEQS_PER_STEP = 8
PAGES_PER_GROUP = 16
DECODE_SLOTS = 4
VMEM_LIMIT = 48 * 1024 * 1024
FFN_VMEM_LIMIT = 56 * 1024 * 1024


def _cparams(sem):
    return pltpu.CompilerParams(dimension_semantics=sem, vmem_limit_bytes=VMEM_LIMIT)


def _rms(x, g):
    return x * lax.rsqrt(jnp.mean(x * x, axis=-1, keepdims=True) + EPS) * g


def _sigmoid(x):
    return 1.0 / (1.0 + jnp.exp(-x))


def _log_sigmoid(x):
    return jnp.minimum(x, 0.0) - jnp.log(1.0 + jnp.exp(-jnp.abs(x)))


def _split2(x):
    hi = x.astype(BF16)
    lo = (x - hi.astype(F32)).astype(BF16)
    return hi, lo


def _split3(x):
    hi = x.astype(BF16)
    r = x - hi.astype(F32)
    mid = r.astype(BF16)
    lo = (r - mid.astype(F32)).astype(BF16)
    return hi, mid, lo


def _dot(a, b):
    return jnp.dot(a, b, preferred_element_type=F32)


def _dot_nt(a, b):
    return lax.dot_general(a, b, (((1,), (1,)), ((), ())), preferred_element_type=F32)


def _dot_tn(a, b):
    return lax.dot_general(a, b, (((0,), (0,)), ((), ())), preferred_element_type=F32)


def _ffn_body(*refs, TF, MIX):
    if MIX:
        x_in_ref, og_ref, osb_ref, y_ref, wout_ref, gmix_ref = refs[:6]
        gpre_ref, gpost_ref, wg_ref, wu_ref, wd_ref, o_ref, h_scr, acc_scr = refs[6:]
        m = _dot(og_ref[...].astype(BF16), wout_ref[0:GLA_V, :])
        m = m + _dot(osb_ref[...].astype(BF16), wout_ref[GLA_V:GLA_V + SB_W, :])
        m = m + _dot(y_ref[...].astype(BF16), wout_ref[GLA_V + SB_W:, :])
        o_ref[...] = x_in_ref[...] + _rms(m, gmix_ref[...])
        x_ref = o_ref
    else:
        x_ref, gpre_ref, gpost_ref, wg_ref, wu_ref, wd_ref, o_ref, h_scr, acc_scr = refs
    h_scr[...] = _rms(x_ref[...], gpre_ref[...]).astype(BF16)
    nf = wg_ref.shape[1] // TF

    def chunk(f):
        cols = pl.ds(pl.multiple_of(f * TF, TF), TF)
        h = h_scr[...]
        g = _dot(h, wg_ref[:, cols])
        u = _dot(h, wu_ref[:, cols])
        a = (g * _sigmoid(g)) * u
        return _dot(a.astype(BF16), wd_ref[cols, :])

    acc_scr[...] = chunk(0)

    def body(f, carry):
        acc_scr[...] += chunk(f)
        return carry

    lax.fori_loop(1, nf, body, 0, unroll=2 if (nf - 1) % 2 == 0 else 1)
    o_ref[...] = x_ref[...] + 0.5 * _rms(acc_scr[...], gpost_ref[...])


def _ffn_stream_body(*refs, MIX):
    if MIX:
        x_in_ref, og_ref, osb_ref, y_ref, wout_ref, gmix_ref = refs[:6]
        refs = refs[6:]
    else:
        x_in_ref, refs = refs[0], refs[1:]
    gpre_ref, gpost_ref, wg_ref, wu_ref, wd_ref, o_ref, wg_out, wu_out, wd_out, h_scr, acc_scr = refs
    f = pl.program_id(1)

    @pl.when(f == 0)
    def _():
        x = x_in_ref[...]
        if MIX:
            m = _dot(og_ref[...].astype(BF16), wout_ref[0:GLA_V, :])
            m = m + _dot(osb_ref[...].astype(BF16), wout_ref[GLA_V:GLA_V + SB_W, :])
            m = m + _dot(y_ref[...].astype(BF16), wout_ref[GLA_V + SB_W:, :])
            x = x + _rms(m, gmix_ref[...])
        o_ref[...] = x
        h_scr[...] = _rms(x, gpre_ref[...]).astype(BF16)
        acc_scr[...] = jnp.zeros_like(acc_scr)

    wg, wu, wd = wg_ref[...].astype(BF16), wu_ref[...].astype(BF16), wd_ref[...].astype(BF16)
    wg_out[...] = wg
    wu_out[...] = wu
    wd_out[...] = wd
    h = h_scr[...]
    g = _dot(h, wg)
    u = _dot(h, wu)
    acc_scr[...] += _dot(((g * _sigmoid(g)) * u).astype(BF16), wd)

    @pl.when(f == pl.num_programs(1) - 1)
    def _():
        o_ref[...] = o_ref[...] + 0.5 * _rms(acc_scr[...], gpost_ref[...])


def _ffn_stream(x, g_pre, g_post, w_gate, w_up, w_down, layer, *, tm, tf, mix=None):
    n, d = x.shape
    hidden = w_gate.shape[2]
    const = lambda a: pl.BlockSpec((None,) + a.shape[1:], lambda i, f: (layer, 0, 0))
    gain = pl.BlockSpec((None, 1, d), lambda i, f: (layer, 0, 0))
    tok = lambda w: pl.BlockSpec((tm, w), lambda i, f: (i, 0))
    in_specs, args = [tok(d)], [x]
    if mix is not None:
        og, osb, y, w_out, g_mix = mix
        in_specs += [tok(GLA_V), tok(SB_W), tok(CONV_CH), const(w_out), gain]
        args += [og, osb, y, w_out, g_mix]
    in_specs += [gain, gain,
                 pl.BlockSpec((None, d, tf), lambda i, f: (layer, 0, f)),
                 pl.BlockSpec((None, d, tf), lambda i, f: (layer, 0, f)),
                 pl.BlockSpec((None, tf, d), lambda i, f: (layer, f, 0))]
    args += [g_pre, g_post, w_gate, w_up, w_down]
    return pl.pallas_call(
        functools.partial(_ffn_stream_body, MIX=mix is not None),
        grid=(n // tm, hidden // tf),
        in_specs=in_specs,
        out_specs=[tok(d), pl.BlockSpec((d, tf), lambda i, f: (0, f)), pl.BlockSpec((d, tf), lambda i, f: (0, f)),
                   pl.BlockSpec((tf, d), lambda i, f: (f, 0))],
        out_shape=[jax.ShapeDtypeStruct((n, d), F32), jax.ShapeDtypeStruct((d, hidden), BF16),
                   jax.ShapeDtypeStruct((d, hidden), BF16), jax.ShapeDtypeStruct((hidden, d), BF16)],
        scratch_shapes=[pltpu.VMEM((tm, d), BF16), pltpu.VMEM((tm, d), F32)],
        compiler_params=_cparams(("arbitrary", "arbitrary")),
        name="half_ffn_stream" if mix is None else "mixer_out_ffn_stream",
    )(*args)


def _ffn(x, g_pre, g_post, w_gate, w_up, w_down, layer, *, tm, tf, mix=None):
    n, d = x.shape
    resident = lambda a: pl.BlockSpec(a.shape, lambda i: (0, 0), pipeline_mode=pl.Buffered(1))
    gain = pl.BlockSpec((None, 1, d), lambda i: (layer, 0, 0))
    tok = lambda w: pl.BlockSpec((tm, w), lambda i: (i, 0))
    in_specs, args = [tok(d)], [x]
    if mix is not None:
        og, osb, y, w_out, g_mix = mix
        in_specs += [tok(GLA_V), tok(SB_W), tok(CONV_CH),
                     pl.BlockSpec((None,) + w_out.shape[1:], lambda i: (layer, 0, 0), pipeline_mode=pl.Buffered(1)),
                     gain]
        args += [og, osb, y, w_out, g_mix]
    in_specs += [gain, gain, resident(w_gate), resident(w_up), resident(w_down)]
    args += [g_pre, g_post, w_gate, w_up, w_down]
    return pl.pallas_call(
        functools.partial(_ffn_body, TF=tf, MIX=mix is not None),
        grid=(n // tm,),
        in_specs=in_specs,
        out_specs=tok(d),
        out_shape=jax.ShapeDtypeStruct((n, d), F32),
        scratch_shapes=[pltpu.VMEM((tm, d), BF16), pltpu.VMEM((tm, d), F32)],
        compiler_params=pltpu.CompilerParams(dimension_semantics=("parallel",),
                                             vmem_limit_bytes=FFN_VMEM_LIMIT if mix is not None else VMEM_LIMIT),
        name="half_ffn" if mix is None else "mixer_out_ffn",
    )(*args)


_SEG = {}
_off = 0
for _name, _w in (("gq", GLA_QK), ("gk", GLA_QK), ("gv", GLA_V), ("gg", GLA_V), ("sq", SB_W), ("sk", SB_W),
                  ("sv", SB_W), ("ca", CONV_CH), ("cg", CONV_CH)):
    _SEG[_name] = (_off, _off + _w)
    _off += _w
MAIN_COLS = _off


def _mixin_body(x_ref, g_ref, w_ref, wlr_ref, wa2_ref, ba_ref, *rest, KV_T, KV_PREV, LAYER):
    if KV_T:
        wkvt_ref, rest = rest[0], rest[1:]
    if KV_PREV:
        rest = rest[2:]
    gq_ref, gk_ref, gv_ref, gg_ref, la_ref, sq_ref, sk_ref, sv_ref, u_ref = rest
    h = _rms(x_ref[...], g_ref[...]).astype(BF16)

    def seg(name):
        lo, hi = _SEG[name]
        return _dot(h, w_ref[:, lo:hi])

    gq_ref[...] = seg("gq") * (GLA_DK ** -0.5)
    gk_ref[...] = seg("gk")
    gv_ref[...] = seg("gv")
    gg_ref[...] = seg("gg")
    sq_ref[...] = seg("sq")
    if KV_T and KV_PREV:
        sk_ref[0] = _dot_nt(wkvt_ref[0:SB_W, :], h)
        sv_ref[0] = _dot_nt(wkvt_ref[SB_W:2 * SB_W, :], h)
    elif KV_T:
        for other in range(sk_ref.shape[0]):
            if other != LAYER:
                sk_ref[other] = jnp.zeros(sk_ref.shape[1:], F32)
                sv_ref[other] = jnp.zeros(sv_ref.shape[1:], F32)
        sk_ref[LAYER, 0] = _dot_nt(wkvt_ref[0:SB_W, :], h)
        sv_ref[LAYER, 0] = _dot_nt(wkvt_ref[SB_W:2 * SB_W, :], h)
    else:
        sk_ref[...] = seg("sk")
        sv_ref[...] = seg("sv")
    u_ref[...] = seg("ca") * _sigmoid(seg("cg"))
    lr = _dot(h, wlr_ref[...])
    xa = _dot(lr.astype(BF16), wa2_ref[...]) + ba_ref[...]
    la_ref[...] = _log_sigmoid(xa) * (1.0 / GLA_TAU)


def _mixin(x, g, w_main, w_lr, w_a2, b_a, w_kvt, *, tm, seq_len, kv_t, layer=0, depth=1, kv_prev=None):
    aliases = {}
    n, d = x.shape
    widths = (GLA_QK, GLA_QK, GLA_V, GLA_V, GLA_QK, SB_W, SB_W, SB_W, CONV_CH)
    const = lambda shape: pl.BlockSpec(shape, lambda i: (0, 0))
    in_specs = [pl.BlockSpec((tm, d), lambda i: (i, 0)), const((1, d)), const(w_main.shape),
                const(w_lr.shape), const(w_a2.shape), const((1, GLA_QK))]
    out_specs = [pl.BlockSpec((tm, w), lambda i: (i, 0)) for w in widths]
    out_shape = [jax.ShapeDtypeStruct((n, w), F32) for w in widths]
    args = [x, g, w_main, w_lr, w_a2, b_a]
    if kv_t:
        assert seq_len % tm == 0
        per_seq = seq_len // tm
        in_specs.append(const(w_kvt.shape))
        args.append(w_kvt)
        for idx in (6, 7):
            if kv_prev is None:
                out_specs[idx] = pl.BlockSpec((depth, 1, SB_W, tm), lambda i: (0, i // per_seq, 0, i % per_seq))
            else:
                out_specs[idx] = pl.BlockSpec((None, 1, SB_W, tm), lambda i: (layer, i // per_seq, 0, i % per_seq))
            out_shape[idx] = jax.ShapeDtypeStruct((depth, n // seq_len, SB_W, seq_len), F32)
        if kv_prev is not None:
            aliases = {len(args): 6, len(args) + 1: 7}
            in_specs += [pl.BlockSpec(memory_space=pl.ANY)] * 2
            args += list(kv_prev)
    return pl.pallas_call(
        functools.partial(_mixin_body, KV_T=kv_t, KV_PREV=bool(aliases), LAYER=layer),
        grid=(n // tm,),
        in_specs=in_specs,
        out_specs=out_specs,
        out_shape=out_shape,
        input_output_aliases=aliases,
        compiler_params=_cparams(("parallel",)),
        name="mixer_in",
    )(*args)


def _gla_body(q_ref, k_ref, v_ref, la_ref, gg_ref, s0_ref, gn_ref, o_ref, sout_ref, st_scr, b_scr, oi_scr, qbs_scr,
              *, C, NB):
    c = pl.program_id(1)

    @pl.when(c == 0)
    def _():
        st_scr[...] = s0_ref[...]

    row = lax.broadcasted_iota(jnp.int32, (C, C), 0)
    col = lax.broadcasted_iota(jnp.int32, (C, C), 1)
    causal = row >= col
    tri = jnp.where(causal, 1.0, 0.0).astype(BF16)
    lane_head = lax.broadcasted_iota(jnp.int32, (C, GLA_QK), 1) // GLA_DK
    heads = [slice(h * GLA_DV, (h + 1) * GLA_DV) for h in range(GLA_HEADS)]

    def head_stack(x):
        return jnp.concatenate([jnp.where(lane_head == h, x, 0.0) for h in range(GLA_HEADS)], axis=0).astype(BF16)

    rows_t = lax.broadcasted_iota(jnp.int32, (GLA_HEADS * C, C), 0) % C
    causal_rows = rows_t >= lax.broadcasted_iota(jnp.int32, (GLA_HEADS * C, C), 1)

    decays = []
    for n in range(NB):
        q, k, v = q_ref[n], k_ref[n], v_ref[n]
        hi, mid, lo = _split3(la_ref[n])
        b = _dot(tri, jnp.concatenate([hi, mid, lo], axis=1))
        b = b[:, :GLA_QK] + b[:, GLA_QK:2 * GLA_QK] + b[:, 2 * GLA_QK:]
        b_scr[n] = b
        b_last = b[C - 1:C, :]
        decays.append(jnp.max(-b_last))
        st = st_scr[n]
        qb = q * jnp.exp(b)
        kd = k * jnp.exp(b_last - b)
        qbs = head_stack(qb)
        qbs_scr[n] = qbs
        oi = _dot_nt(qbs, st.astype(BF16))
        for h, sl in enumerate(heads):
            oi_scr[n, :, sl] = oi[h * C:(h + 1) * C]
        v_rows = jnp.concatenate([v[:, sl] for sl in heads], axis=0).astype(BF16)
        st_scr[n] = st * jnp.exp(b_last) + _dot_tn(v_rows, head_stack(kd))
    total_decay = functools.reduce(jnp.maximum, decays)

    @pl.when(total_decay <= GLA_SAFE_DECAY)
    def _():
        for n in range(NB):
            k, v, b = k_ref[n], v_ref[n], b_scr[n]
            kb = (k * jnp.exp(-b)).astype(BF16)
            attn = jnp.where(causal_rows, _dot_nt(qbs_scr[n], kb), 0.0).astype(BF16)
            for h, sl in enumerate(heads):
                oi_scr[n, :, sl] += _dot(attn[h * C:(h + 1) * C], v[:, sl].astype(BF16))

    @pl.when(total_decay > GLA_SAFE_DECAY)
    def _():
        rhead = lax.broadcasted_iota(jnp.int32, (GLA_QK, GLA_V), 0) // GLA_DK
        chead = lax.broadcasted_iota(jnp.int32, (GLA_QK, GLA_V), 1) // GLA_DV
        spread = jnp.where(rhead == chead, 1.0, 0.0).astype(BF16)
        trow = lax.broadcasted_iota(jnp.int32, (C, GLA_QK), 0)
        for n in range(NB):
            q, b = q_ref[n], b_scr[n]

            def body(s, acc):
                bs = b_scr[n, pl.ds(s, 1), :]
                ks = k_ref[n, pl.ds(s, 1), :]
                vs = v_ref[n, pl.ds(s, 1), :]
                p = q * ks * jnp.exp(jnp.minimum(b - bs, 0.0))
                p = jnp.where(trow >= s, p, 0.0)
                p_hi, p_mid, p_lo = _split3(p)
                r = _dot(p_hi, spread) + _dot(p_mid, spread) + _dot(p_lo, spread)
                return acc + r * vs

            oi_scr[n] += lax.fori_loop(0, C, body, jnp.zeros((C, GLA_V), F32))

    for n in range(NB):
        o, gg = oi_scr[n], gg_ref[n]
        for sl in heads:
            o_ref[n, :, sl] = _rms(o[:, sl], gn_ref[:, sl]) * (gg[:, sl] * _sigmoid(gg[:, sl]))

    @pl.when(c == pl.num_programs(1) - 1)
    def _():
        sout_ref[...] = st_scr[...]


def _gla(q, k, v, la, gg, s0_t, gnorm):
    bsz, t_real, _ = q.shape
    if t_real < 16:
        padt = lambda a: jnp.pad(a, ((0, 0), (0, 16 - t_real), (0, 0)))
        q, k, v, la, gg = (padt(a) for a in (q, k, v, la, gg))
    t = q.shape[1]
    c = min(GLA_CHUNK, t)
    nb = GLA_SEQS_PER_STEP if bsz % GLA_SEQS_PER_STEP == 0 else 1
    assert t % c == 0
    tok = lambda w: pl.BlockSpec((nb, c, w), lambda b, i: (b, i, 0))
    st = pl.BlockSpec((nb, GLA_DV, GLA_QK), lambda b, i: (b, 0, 0))
    o, st_new = pl.pallas_call(
        functools.partial(_gla_body, C=c, NB=nb),
        grid=(bsz // nb, t // c),
        in_specs=[tok(GLA_QK), tok(GLA_QK), tok(GLA_V), tok(GLA_QK), tok(GLA_V), st,
                  pl.BlockSpec((1, GLA_V), lambda b, i: (0, 0))],
        out_specs=[tok(GLA_V), st],
        out_shape=[jax.ShapeDtypeStruct((bsz, t, GLA_V), F32), jax.ShapeDtypeStruct((bsz, GLA_DV, GLA_QK), F32)],
        scratch_shapes=[pltpu.VMEM((nb, GLA_DV, GLA_QK), F32), pltpu.VMEM((nb, c, GLA_QK), F32),
                        pltpu.VMEM((nb, c, GLA_V), F32), pltpu.VMEM((nb, GLA_HEADS * c, GLA_QK), BF16)],
        compiler_params=_cparams(("parallel", "arbitrary")),
        name="gla",
    )(q, k, v, la, gg, s0_t, gnorm)
    return o[:, :t_real], st_new


def _state_to_t(s):
    b = s.shape[0]
    return jnp.transpose(s, (0, 3, 1, 2)).reshape(b, GLA_DV, GLA_QK)


def _state_from_t(st):
    b = st.shape[0]
    return jnp.transpose(st.reshape(b, GLA_DV, GLA_HEADS, GLA_DK), (0, 2, 3, 1))


def _sb_weights(z, tail_mat_ref, carry, mask):
    ls = _log_sigmoid(z)
    l1m = ls - z
    if mask is not None:
        l1m = jnp.where(mask, l1m, 0.0)
    hi, lo = _split2(l1m)
    cs = _dot(jnp.concatenate([hi, lo], axis=1), tail_mat_ref[...])
    w = jnp.exp(ls + cs[:, :LANES] + carry)
    if mask is not None:
        w = jnp.where(mask, w, 0.0)
    return w, carry + cs[:, LANES:]


def _tail_matrix():
    src = jnp.arange(2 * LANES)[:, None] % LANES
    dst = jnp.arange(2 * LANES)[None, :]
    return jnp.where((dst >= LANES) | (src > dst), 1.0, 0.0).astype(BF16)


def _sb_prompt_body(bias_ref, q_ref, k_ref, v_ref, tm_ref, o_ref, kbf_scr, vbd_scr, qbd_scr, carry_scr, acc_scr,
                    *, T):
    i = pl.program_id(1)
    nkb = T // SB_TILE
    R = SB_HEADS * SB_TILE
    lane_head = lax.broadcasted_iota(jnp.int32, (SB_TILE, SB_W), 1) // SB_DIM

    @pl.when(i == 0)
    def _():
        kbf_scr[...] = k_ref[0].astype(BF16)
        for jb in range(nkb):
            vb = v_ref[0, :, jb * SB_TILE:(jb + 1) * SB_TILE].T
            for h in range(SB_HEADS):
                vbd_scr[jb, h * SB_TILE:(h + 1) * SB_TILE, :] = jnp.where(lane_head == h, vb, 0.0).astype(BF16)

    q = q_ref[0] * (SB_DIM ** -0.5)
    for h in range(SB_HEADS):
        qbd_scr[h * SB_TILE:(h + 1) * SB_TILE, :] = jnp.where(lane_head == h, q, 0.0).astype(BF16)
    carry_scr[...] = jnp.zeros_like(carry_scr)
    acc_scr[...] = jnp.zeros_like(acc_scr)

    def tile(j, mask):
        kblk = kbf_scr[:, pl.ds(pl.multiple_of(j * SB_TILE, SB_TILE), SB_TILE)]
        z = _dot(qbd_scr[...], kblk) + bias_ref[...]
        w, carry_scr[...] = _sb_weights(z, tm_ref, carry_scr[...], mask)
        wb = w.astype(BF16)
        wcat = jnp.concatenate([wb[h * SB_TILE:(h + 1) * SB_TILE] for h in range(SB_HEADS)], axis=1)
        acc_scr[...] += _dot(wcat, vbd_scr[j])

    trow = lax.broadcasted_iota(jnp.int32, (R, SB_TILE), 0) % SB_TILE
    scol = lax.broadcasted_iota(jnp.int32, (R, SB_TILE), 1)
    rem = i % SB_TILES_PER_TRIP
    for r in range(SB_TILES_PER_TRIP):
        @pl.when(rem == r)
        def _(r=r):
            tile(i, scol < trow)
            for u in range(r):
                tile(i - 1 - u, None)

    def group(jj, _):
        for u in range(SB_TILES_PER_TRIP):
            tile(i - rem - 1 - u - SB_TILES_PER_TRIP * jj, None)
        return 0

    lax.fori_loop(0, i // SB_TILES_PER_TRIP, group, 0)

    o_ref[0] = acc_scr[...]


def _sb_prompt(q, k_t, v_t, bias_rows, tail_mat, layer):
    bsz, t, w = q.shape
    nq = t // SB_TILE
    full = pl.BlockSpec((None, 1, w, t), lambda b, i: (layer, b, 0, 0))
    qt = pl.BlockSpec((1, SB_TILE, w), lambda b, i: (b, i, 0))
    const = lambda a: pl.BlockSpec(a.shape, lambda b, i: (0, 0))
    return pl.pallas_call(
        functools.partial(_sb_prompt_body, T=t),
        grid=(bsz, nq),
        in_specs=[const(bias_rows), qt, full, full, const(tail_mat)],
        out_specs=qt,
        out_shape=jax.ShapeDtypeStruct((bsz, t, w), F32),
        scratch_shapes=[pltpu.VMEM((w, t), BF16), pltpu.VMEM((nq, SB_HEADS * SB_TILE, w), BF16),
                        pltpu.VMEM((SB_HEADS * SB_TILE, w), BF16), pltpu.VMEM((SB_HEADS * SB_TILE, LANES), F32),
                        pltpu.VMEM((SB_TILE, w), F32)],
        compiler_params=_cparams(("parallel", "arbitrary")),
        name="sb_prompt",
    )(bias_rows, q, k_t, v_t, tail_mat)


def _sb_decode_body(pt_ref, bias_ref, q_ref, kn_ref, vn_ref, tm_ref, kc_hbm, vc_hbm, o_ref,
                    kbuf, vbuf, sem, knew_scr, vnew_scr, carry_scr, acc_scr, *, TQ, NP, G, NB, NSLOT, LAYER):
    b = pl.program_id(0)
    nb = NB
    ng = NP // G
    R = SB_HEADS * TQ

    def copies(bb, grp, slot):
        out = []
        for p in range(G):
            page = pt_ref[bb, grp * G + p]
            keys = pl.ds(p * SB_TILE, SB_TILE)
            out.append(pltpu.make_async_copy(kc_hbm.at[LAYER, page], kbuf.at[slot, :, keys], sem.at[0, slot]))
            out.append(pltpu.make_async_copy(vc_hbm.at[LAYER, page], vbuf.at[slot, :, keys], sem.at[1, slot]))
        return out

    def start_group(step):
        for cp in copies(step // ng, ng - 1 - step % ng, step % NSLOT):
            cp.start()

    @pl.when(b == 0)
    def _():
        for s in range(min(NSLOT - 1, nb * ng)):
            start_group(s)

    lane_head = lax.broadcasted_iota(jnp.int32, (TQ, SB_W), 1) // SB_DIM
    q = q_ref[0] * (SB_DIM ** -0.5)
    qbd = jnp.concatenate([jnp.where(lane_head == h, q, 0.0) for h in range(SB_HEADS)], axis=0).astype(BF16)
    bias = jnp.concatenate([jnp.full((TQ, LANES), bias_ref[h], F32) for h in range(SB_HEADS)], axis=0)

    knew_scr[...] = jnp.zeros_like(knew_scr)
    vnew_scr[...] = jnp.zeros_like(vnew_scr)
    knew_scr[0:TQ, :] = kn_ref[0]
    vnew_scr[0:TQ, :] = vn_ref[0]
    trow = lax.broadcasted_iota(jnp.int32, (R, LANES), 0) % TQ
    scol = lax.broadcasted_iota(jnp.int32, (R, LANES), 1)
    z = _dot_nt(qbd, knew_scr[...].astype(BF16)) + bias
    w, carry = _sb_weights(z, tm_ref, jnp.zeros((R, LANES), F32), scol < trow)
    carry_scr[...] = carry
    acc_scr[...] = _dot(w.astype(BF16), vnew_scr[...].astype(BF16))

    def group(r, _):
        step = b * ng + r
        slot = step % NSLOT
        for cp in copies(b, ng - 1 - r, slot):
            cp.wait()

        @pl.when(step + NSLOT - 1 < nb * ng)
        def _():
            start_group(step + NSLOT - 1)

        kg = kbuf[slot].astype(BF16)
        vg = vbuf[slot].astype(BF16)
        zg = _dot(qbd, kg)
        zrows = jnp.concatenate([zg[:, p * LANES:(p + 1) * LANES] + bias for p in range(G)], axis=0)
        ls = _log_sigmoid(zrows)
        hi, lo = _split2(ls - zrows)
        cs = _dot(jnp.concatenate([hi, lo], axis=1), tm_ref[...])
        carry = carry_scr[...]
        ws = [None] * G
        for p in reversed(range(G)):
            rs = slice(p * R, (p + 1) * R)
            ws[p] = jnp.exp(ls[rs] + cs[rs, :LANES] + carry).astype(BF16)
            carry = carry + cs[rs, LANES:]
        carry_scr[...] = carry
        acc_scr[...] += _dot_nt(jnp.concatenate(ws, axis=1), vg)
        return 0

    lax.fori_loop(0, ng, group, 0)
    acc = acc_scr[...]
    out = jnp.zeros((TQ, SB_W), F32)
    for h in range(SB_HEADS):
        out = out + jnp.where(lane_head == h, acc[h * TQ:(h + 1) * TQ, :], 0.0)
    o_ref[0] = out


def _sb_decode(q, k_new, v_new, bias, tail_mat, cache_k, cache_v, page_table, layer):
    db, tq, w = q.shape
    npages = page_table.shape[1]
    g = math.gcd(npages, PAGES_PER_GROUP)
    assert cache_k.shape[2:] == (w, SB_TILE)
    tok = pl.BlockSpec((1, tq, w), lambda b, pt: (b, 0, 0))
    grid_spec = pltpu.PrefetchScalarGridSpec(
        num_scalar_prefetch=1,
        grid=(db,),
        in_specs=[pl.BlockSpec(memory_space=pltpu.SMEM), tok, tok, tok,
                  pl.BlockSpec(tail_mat.shape, lambda b, pt: (0, 0)),
                  pl.BlockSpec(memory_space=pl.ANY), pl.BlockSpec(memory_space=pl.ANY)],
        out_specs=tok,
        scratch_shapes=[pltpu.VMEM((DECODE_SLOTS, w, g * SB_TILE), F32), pltpu.VMEM((DECODE_SLOTS, w, g * SB_TILE), F32),
                        pltpu.SemaphoreType.DMA((2, DECODE_SLOTS)),
                        pltpu.VMEM((SB_TILE, w), F32), pltpu.VMEM((SB_TILE, w), F32),
                        pltpu.VMEM((SB_HEADS * tq, LANES), F32), pltpu.VMEM((SB_HEADS * tq, w), F32)],
    )
    return pl.pallas_call(
        functools.partial(_sb_decode_body, TQ=tq, NP=npages, G=g, NB=db, NSLOT=DECODE_SLOTS,
                          LAYER=layer),
        grid_spec=grid_spec,
        out_shape=jax.ShapeDtypeStruct((db, tq, w), F32),
        compiler_params=_cparams(("arbitrary",)),
        name="sb_decode",
    )(page_table, bias, q, k_new, v_new, tail_mat, cache_k, cache_v)


def _conv_body(u_ref, prev_ref, buf_ref, w_ref, cb_ref, lg_ref, lb_ref, y_ref, bufo_ref, win_scr, *, RT, NT):
    i = pl.program_id(1)
    hist = CONV_WIDTH - 1
    first = HALO - hist

    @pl.when(i == 0)
    def _():
        win_scr[first:HALO, :] = buf_ref[0]

    if NT > 1:
        @pl.when(i > 0)
        def _():
            win_scr[0:HALO, :] = prev_ref[0]

    win_scr[HALO:HALO + RT, :] = u_ref[0]
    st = min(RT, CONV_SUBTILE)
    n = HALO + st
    for base in range(0, RT, st):
        win = win_scr[base:base + n, :]
        acc = jnp.zeros((st, CONV_CH), F32)
        for r in range(SUBLANES):
            offs = [o for o in range(first, first + CONV_WIDTH) if o % SUBLANES == r]
            shifted = win if r == 0 else pltpu.roll(win, n - r, 0)
            for o in offs:
                acc = acc + shifted[o - r:o - r + st, :] * w_ref[o - first:o - first + 1, :]
        y = acc + cb_ref[...]
        mu = jnp.mean(y, axis=-1, keepdims=True)
        yc = y - mu
        yn = yc * lax.rsqrt(jnp.mean(yc * yc, axis=-1, keepdims=True) + EPS) * lg_ref[...] + lb_ref[...]
        y_ref[0, base:base + st, :] = yn * _sigmoid(yn)

    @pl.when(i == pl.num_programs(1) - 1)
    def _():
        bufo_ref[0] = win_scr[first + RT:HALO + RT, :]


def _conv(u, buf, w, cb, lg, lb):
    bsz, t, ch = u.shape
    hist = CONV_WIDTH - 1
    rt = min(t, CONV_TILE)
    nt = t // rt
    assert t % rt == 0 and (nt == 1 or rt % HALO == 0)
    per = rt // HALO if nt > 1 else 1
    tile = pl.BlockSpec((1, rt, ch), lambda b, i: (b, i, 0))
    prev_rows = HALO if nt > 1 else rt
    prev = pl.BlockSpec((1, prev_rows, ch), lambda b, i: (b, jnp.maximum(i * per - 1, 0), 0))
    state = pl.BlockSpec((1, hist, ch), lambda b, i: (b, 0, 0))
    vec = lambda r: pl.BlockSpec((r, ch), lambda b, i: (0, 0))
    return pl.pallas_call(
        functools.partial(_conv_body, RT=rt, NT=nt),
        grid=(bsz, nt),
        in_specs=[tile, prev, state, vec(CONV_WIDTH), vec(1), vec(1), vec(1)],
        out_specs=[tile, state],
        out_shape=[jax.ShapeDtypeStruct((bsz, t, ch), F32), jax.ShapeDtypeStruct((bsz, hist, ch), F32)],
        scratch_shapes=[pltpu.VMEM((HALO + rt, ch), F32)],
        compiler_params=_cparams(("parallel", "arbitrary")),
        name="conv_branch",
    )(u, u, buf, w, cb, lg, lb)


def _layer_weights(l, w):
    d = w["w_in"].shape[1]
    w_in = w["w_in"][l]
    sizes = (GLA_QK, GLA_QK, GLA_V, GLA_V, GLA_RANK, SB_W, SB_W, SB_W, 2 * CONV_CH)
    offs = [0]
    for s in sizes:
        offs.append(offs[-1] + s)
    main = jnp.concatenate([w_in[:, :offs[4]], w_in[:, offs[5]:]], axis=1).astype(BF16)
    w_lr = jnp.zeros((d, LANES), F32).at[:, :GLA_RANK].set(w_in[:, offs[4]:offs[5]]).astype(BF16)
    w_a2 = jnp.zeros((LANES, GLA_QK), F32).at[:GLA_RANK, :].set(w["gla_w_a2"][l]).astype(BF16)
    row = lambda name: w[name][l][None, :]
    return dict(
        layer=l,
        ffn1_gains=(w["ffn1_norm_pre"][:, None, :], w["ffn1_norm_post"][:, None, :]),
        ffn2_gains=(w["ffn2_norm_pre"][:, None, :], w["ffn2_norm_post"][:, None, :]),
        ffn1_f32=(w["ffn1_w_gate"], w["ffn1_w_up"], w["ffn1_w_down"]),
        ffn2_f32=(w["ffn2_w_gate"], w["ffn2_w_up"], w["ffn2_w_down"]),
        mixin=(row("mix_norm_pre"), main, w_lr, w_a2, row("gla_b_a"), w_in[:, offs[6]:offs[8]].T.astype(BF16)),
        gla_norm=row("gla_norm"),
        sb_bias=w["sb_bias"][l],
        sb_bias_rows=jnp.broadcast_to(jnp.repeat(w["sb_bias"][l], SB_TILE)[:, None], (SB_HEADS * SB_TILE, LANES)),
        conv=(w["conv_w"][l], row("conv_b"), row("conv_norm_g"), row("conv_norm_b")),
        w_out=w["w_out_bf"],
        mix_post=w["mix_norm_post"][:, None, :],
    )


def _token_tile(n):
    for tm in (1024, 512, 256, 128, 64, 32, 16, 8):
        if n % tm == 0:
            return tm
    raise ValueError(n)


def _ffn_tile(f):
    for tf in (256, 128):
        if f % tf == 0:
            return tf
    return f


def _group_layer(x, lw, s0, buf, tail_mat, attend, kv_t, ffn_bf):
    bsz, t, d = x.shape
    n = bsz * t
    tm = _token_tile(n)
    tf = _ffn_tile(lw["ffn1_f32"][0].shape[2])
    made_bf = {}

    def ffn(which, xin, mix=None):
        if ffn_bf is not None:
            return _ffn(xin, *lw[which + "_gains"], *ffn_bf[which], lw["layer"], tm=tm, tf=tf, mix=mix)
        out, *made_bf[which] = _ffn_stream(xin, *lw[which + "_gains"], *lw[which + "_f32"], lw["layer"], tm=tm, tf=tf,
                                           mix=mix)
        return out

    x2 = ffn("ffn1", x.reshape(n, d))
    gq, gk, gv, gg, la, sq, sk, sv, u = _mixin(x2, *lw["mixin"], tm=min(tm, 512, t) if kv_t else min(tm, 512),
                                               seq_len=t, kv_t=kv_t, layer=lw["layer"], depth=lw["depth"],
                                               kv_prev=lw["kv_prev"] if kv_t else None)
    seq = lambda a: a.reshape(bsz, t, a.shape[-1])
    o_gla, st = _gla(seq(gq), seq(gk), seq(gv), seq(la), seq(gg), _state_to_t(s0), lw["gla_norm"])
    o_sb = attend(seq(sq), sk, sv) if kv_t else attend(seq(sq), seq(sk), seq(sv))
    y, buf_new = _conv(seq(u), buf, *lw["conv"])
    mix = (o_gla.reshape(n, GLA_V), o_sb.reshape(n, SB_W), y.reshape(n, CONV_CH), lw["w_out"], lw["mix_post"])
    x4 = ffn("ffn2", x2, mix)
    if not kv_t:
        sk, sv = (a.reshape(bsz, t, SB_HEADS, SB_DIM) for a in (sk, sv))
    return x4.reshape(bsz, t, d), sk, sv, _state_from_t(st), buf_new, made_bf


def kernel(x_prompt, x_sample, cache_sb_k, cache_sb_v, page_table, state_gla, state_conv, ffn1_norm_pre, ffn1_norm_post, ffn1_w_gate, ffn1_w_up, ffn1_w_down, mix_norm_pre, mix_norm_post, w_in, gla_w_a2, gla_b_a, gla_norm, sb_bias, conv_w, conv_b, conv_norm_g, conv_norm_b, w_out, ffn2_norm_pre, ffn2_norm_post, ffn2_w_gate, ffn2_w_up, ffn2_w_down):
    weights = dict(ffn1_norm_pre=ffn1_norm_pre, ffn1_norm_post=ffn1_norm_post, ffn1_w_gate=ffn1_w_gate,
                   ffn1_w_up=ffn1_w_up, ffn1_w_down=ffn1_w_down, mix_norm_pre=mix_norm_pre,
                   mix_norm_post=mix_norm_post, w_in=w_in, gla_w_a2=gla_w_a2, gla_b_a=gla_b_a, gla_norm=gla_norm,
                   sb_bias=sb_bias, conv_w=conv_w, conv_b=conv_b, conv_norm_g=conv_norm_g,
                   conv_norm_b=conv_norm_b, w_out=w_out, ffn2_norm_pre=ffn2_norm_pre,
                   ffn2_norm_post=ffn2_norm_post, ffn2_w_gate=ffn2_w_gate, ffn2_w_up=ffn2_w_up,
                   ffn2_w_down=ffn2_w_down)
    weights["w_out_bf"] = w_out.astype(BF16)
    depth = w_in.shape[0]
    bsz = x_prompt.shape[0]
    n_pool, page = cache_sb_k.shape[1], cache_sb_k.shape[2]
    tail_mat = _tail_matrix()
    ck = jnp.transpose(cache_sb_k, (0, 1, 3, 4, 2)).reshape(depth, n_pool, SB_W, page)
    cv = jnp.transpose(cache_sb_v, (0, 1, 3, 4, 2)).reshape(depth, n_pool, SB_W, page)
    xp, xs = x_prompt, x_sample
    outs = [[] for _ in range(6)]
    prompt_kv = None
    for l in range(depth):
        lw = _layer_weights(l, weights)
        lw.update(depth=depth, kv_prev=prompt_kv)
        s0 = jnp.zeros((bsz, GLA_HEADS, GLA_DK, GLA_DV), F32)
        buf0 = jnp.zeros((bsz, CONV_WIDTH - 1, CONV_CH), F32)
        decode_attend = lambda q, k, v, l=l: _sb_decode(q, k, v, lw["sb_bias"], tail_mat, ck, cv, page_table, l)
        xs, k2, v2, s2, c2, ffn_bf = _group_layer(xs, lw, state_gla[l], state_conv[l], tail_mat, decode_attend, False,
                                                  None)
        prompt_attend = lambda q, k, v, l=l: _sb_prompt(q, k, v, lw["sb_bias_rows"], tail_mat, l)
        xp, k1, v1, s1, c1, _ = _group_layer(xp, lw, s0, buf0, tail_mat, prompt_attend, True, ffn_bf)
        prompt_kv = (k1, v1)
        for lst, val in zip(outs, (s1, c1, k2, v2, s2, c2)):
            lst.append(val)
    seq_len = x_prompt.shape[1]
    pk, pv = (jnp.transpose(a.reshape(depth, bsz, SB_HEADS, SB_DIM, seq_len), (0, 1, 4, 2, 3)) for a in prompt_kv)
    return (xp, xs, pk, pv) + tuple(jnp.stack(o) for o in outs)
```
